```python
import jax, jax.numpy as jnp
from jax import lax
import numpy as np

D_MODEL = 1024
BATCH = 16
SEQ = 2048
DEPTH = 2

N_A_LAYERS = DEPTH // 2
N_B_LAYERS = DEPTH - N_A_LAYERS

D_FF = 2816
MACARON_WEIGHT = 0.5

HG_HEADS = 8
HG_DK = 128
HG_FD = HG_HEADS * HG_DK
HG_DV = D_MODEL // HG_HEADS
HG_CHUNK = 64

ATT_HEADS = 16
ATT_KV_HEADS = 2
ATT_GROUP = ATT_HEADS // ATT_KV_HEADS
ATT_HEAD_DIM = 64
WINDOW = 128
ATT_SCALE = ATT_HEAD_DIM ** -0.5

ROPE_THETA = 500000.0
ROT_DIM = ATT_HEAD_DIM // 4

EPS = 1e-6

kernel_name = "yoco_hgrn2_swa_sink_macaron"


def rms_norm(x, g):
    xf = x.astype(jnp.float32)
    y = xf * lax.rsqrt(jnp.mean(xf * xf, axis=-1, keepdims=True) + EPS)
    return (y * g.astype(jnp.float32)).astype(x.dtype)


def swiglu(h, w_gate_up, w_down):
    gate, up = jnp.split(h @ w_gate_up, 2, axis=-1)
    return (jax.nn.silu(gate) * up) @ w_down


def partial_rope(x, pos):
    half = ROT_DIM // 2
    inv_freq = jnp.power(ROPE_THETA, -jnp.arange(half, dtype=jnp.float32) * (2.0 / ROT_DIM))
    ang = pos.astype(jnp.float32)[:, None] * inv_freq[None, :]
    cos = jnp.cos(ang)[None, :, None, :]
    sin = jnp.sin(ang)[None, :, None, :]
    xf = x.astype(jnp.float32)
    x1 = xf[..., :half]
    x2 = xf[..., half:ROT_DIM]
    out = jnp.concatenate([x1 * cos - x2 * sin, x2 * cos + x1 * sin, xf[..., ROT_DIM:]], axis=-1)
    return out.astype(x.dtype)


def hgrn2_mixer(h, w_in, lower_bound, onorm_g, w_out):
    B, S, _ = h.shape
    n_chunks = S // HG_CHUNK
    proj = h @ w_in
    q, f, i, g = jnp.split(proj, [HG_FD, 2 * HG_FD, 2 * HG_FD + D_MODEL], axis=-1)
    q = jax.nn.silu(q.astype(jnp.float32))
    forget = lower_bound + (1.0 - lower_bound) * jax.nn.sigmoid(f.astype(jnp.float32))
    k = 1.0 - forget
    log_f = jnp.log(forget)

    def to_chunks(t, d):
        return t.reshape(B, n_chunks, HG_CHUNK, HG_HEADS, d).transpose(1, 0, 3, 2, 4)

    qc = to_chunks(q, HG_DK)
    kc = to_chunks(k, HG_DK)
    vc = to_chunks(i.astype(jnp.float32), HG_DV)
    lc = to_chunks(log_f, HG_DK)
    causal = jnp.tril(jnp.ones((HG_CHUNK, HG_CHUNK), dtype=bool))

    def step(state, inp):
        q_t, k_t, v_t, l_t = inp
        b = jnp.cumsum(l_t, axis=-2)
        b_end = b[..., -1:, :]
        q_e = q_t * jnp.exp(b)
        k_e = k_t * jnp.exp(-b)
        k_d = k_t * jnp.exp(b_end - b)
        scores = jnp.einsum('bhtk,bhsk->bhts', q_e, k_e)
        scores = jnp.where(causal, scores, 0.0)
        o = (jnp.einsum('bhtk,bhkv->bhtv', q_e, state)
             + jnp.einsum('bhts,bhsv->bhtv', scores, v_t))
        new_state = (jnp.exp(b_end)[..., 0, :, None] * state
                     + jnp.einsum('bhsk,bhsv->bhkv', k_d, v_t))
        return new_state, o

    state0 = jnp.zeros((B, HG_HEADS, HG_DK, HG_DV), dtype=jnp.float32)
    _, o = lax.scan(step, state0, (qc, kc, vc, lc))
    o = o.transpose(1, 0, 3, 2, 4).reshape(B, S, HG_HEADS, HG_DV)
    gate = jax.nn.silu(g.astype(jnp.float32)).reshape(B, S, HG_HEADS, HG_DV)
    o = rms_norm(o, onorm_g) * gate
    return o.reshape(B, S, D_MODEL).astype(h.dtype) @ w_out


def shared_kv(h, kv_norm_g, kv_w, k_norm_g, pos):
    B, S, _ = h.shape
    hn = rms_norm(h, kv_norm_g)
    k, v = jnp.split(hn @ kv_w, 2, axis=-1)
    k = k.reshape(B, S, ATT_KV_HEADS, ATT_HEAD_DIM)
    k = partial_rope(rms_norm(k, k_norm_g), pos)
    v = v.reshape(B, S, ATT_KV_HEADS, ATT_HEAD_DIM)
    return k, v


def swa_sink_mixer(h, k, v, w_q, q_norm_g, sinks, w_out, pos):
    B, S, _ = h.shape
    n_blocks = S // WINDOW
    q = (h @ w_q).reshape(B, S, ATT_HEADS, ATT_HEAD_DIM)
    q = partial_rope(rms_norm(q, q_norm_g), pos)
    qb = q.reshape(B, n_blocks, WINDOW, ATT_KV_HEADS, ATT_GROUP, ATT_HEAD_DIM).astype(jnp.float32)
    kb = k.reshape(B, n_blocks, WINDOW, ATT_KV_HEADS, ATT_HEAD_DIM).astype(jnp.float32)
    vb = v.reshape(B, n_blocks, WINDOW, ATT_KV_HEADS, ATT_HEAD_DIM).astype(jnp.float32)
    pad = jnp.zeros_like(kb[:, :1])
    k_band = jnp.concatenate([jnp.concatenate([pad, kb[:, :-1]], axis=1), kb], axis=2)
    v_band = jnp.concatenate([jnp.concatenate([pad, vb[:, :-1]], axis=1), vb], axis=2)
    scores = jnp.einsum('bnqkgd,bnskd->bnkgqs', qb, k_band) * ATT_SCALE
    q_pos = jnp.arange(S).reshape(n_blocks, WINDOW)[:, :, None]
    k_pos = (jnp.arange(n_blocks)[:, None] * WINDOW - WINDOW + jnp.arange(2 * WINDOW)[None, :])[:, None, :]
    delta = q_pos - k_pos
    allowed = (delta >= 0) & (delta < WINDOW) & (k_pos >= 0)
    scores = jnp.where(allowed[None, :, None, None], scores, -jnp.inf)
    sink = jnp.broadcast_to(
        sinks.astype(jnp.float32).reshape(ATT_KV_HEADS, ATT_GROUP)[None, None, :, :, None, None],
        scores.shape[:-1] + (1,))
    probs = jax.nn.softmax(jnp.concatenate([scores, sink], axis=-1), axis=-1)[..., :-1]
    o = jnp.einsum('bnkgqs,bnskd->bnqkgd', probs, v_band)
    o = o.reshape(B, S, ATT_HEADS * ATT_HEAD_DIM).astype(h.dtype)
    return o @ w_out


def setup_inputs(seed: int = 0) -> dict:
    key = jax.random.key(seed)
    ks = jax.random.split(key, 16)
    f32 = jnp.float32

    def w(k, shape, fan_in):
        return jax.random.normal(k, shape, f32) * (fan_in ** -0.5)

    def gain(k, shape):
        return 1.0 + 0.01 * jax.random.normal(k, shape, f32)

    return {
        "x": jax.random.normal(ks[0], (BATCH, SEQ, D_MODEL), f32),
        "ffn_norm_g": gain(ks[1], (DEPTH, 2, D_MODEL)),
        "ffn_w_gate_up": w(ks[2], (DEPTH, 2, D_MODEL, 2 * D_FF), D_MODEL),
        "ffn_w_down": w(ks[3], (DEPTH, 2, D_FF, D_MODEL), D_FF),
        "mix_norm_g": gain(ks[4], (DEPTH, D_MODEL)),
        "hgrn_w_in": w(ks[5], (N_A_LAYERS, D_MODEL, 2 * HG_FD + 2 * D_MODEL), D_MODEL),
        "hgrn_lb_logits": 0.1 * jax.random.normal(ks[6], (N_A_LAYERS + 1, HG_FD), f32),
        "hgrn_onorm_g": gain(ks[7], (N_A_LAYERS, HG_DV)),
        "hgrn_w_out": w(ks[8], (N_A_LAYERS, D_MODEL, D_MODEL), D_MODEL),
        "kv_norm_g": gain(ks[9], (D_MODEL,)),
        "kv_w": w(ks[10], (D_MODEL, 2 * ATT_KV_HEADS * ATT_HEAD_DIM), D_MODEL),
        "k_norm_g": gain(ks[11], (ATT_HEAD_DIM,)),
        "attn_w_q": w(ks[12], (N_B_LAYERS, D_MODEL, ATT_HEADS * ATT_HEAD_DIM), D_MODEL),
        "q_norm_g": gain(ks[13], (N_B_LAYERS, ATT_HEAD_DIM)),
        "attn_sinks": jax.random.normal(ks[14], (N_B_LAYERS, ATT_HEADS), f32),
        "attn_w_out": w(ks[15], (N_B_LAYERS, ATT_HEADS * ATT_HEAD_DIM, D_MODEL), ATT_HEADS * ATT_HEAD_DIM),
    }


def reference(x, ffn_norm_g, ffn_w_gate_up, ffn_w_down, mix_norm_g, hgrn_w_in, hgrn_lb_logits,
              hgrn_onorm_g, hgrn_w_out, kv_norm_g, kv_w, k_norm_g, attn_w_q, q_norm_g,
              attn_sinks, attn_w_out):
    pos = jnp.arange(x.shape[1])
    lower_bounds = jnp.cumsum(jax.nn.softmax(hgrn_lb_logits.astype(jnp.float32), axis=0), axis=0)
    h = x
    k_sh = None
    v_sh = None
    for layer in range(DEPTH):
        h = h + MACARON_WEIGHT * swiglu(rms_norm(h, ffn_norm_g[layer, 0]),
                                        ffn_w_gate_up[layer, 0], ffn_w_down[layer, 0])
        hn = rms_norm(h, mix_norm_g[layer])
        if layer < N_A_LAYERS:
            a = layer
            h = h + hgrn2_mixer(hn, hgrn_w_in[a], lower_bounds[a], hgrn_onorm_g[a], hgrn_w_out[a])
        else:
            b = layer - N_A_LAYERS
            h = h + swa_sink_mixer(hn, k_sh, v_sh, attn_w_q[b], q_norm_g[b], attn_sinks[b],
                                   attn_w_out[b], pos)
        h = h + MACARON_WEIGHT * swiglu(rms_norm(h, ffn_norm_g[layer, 1]),
                                        ffn_w_gate_up[layer, 1], ffn_w_down[layer, 1])
        if layer == N_A_LAYERS - 1:
            k_sh, v_sh = shared_kv(h, kv_norm_g, kv_w, k_norm_g, pos)
    return h
```

```python
import functools

import jax
import jax.numpy as jnp
from jax import lax
from jax.experimental import pallas as pl
from jax.experimental.pallas import tpu as pltpu

D_MODEL = 1024
D_FF = 2816
MACARON_WEIGHT = 0.5

HG_HEADS = 8
HG_DK = 128
HG_FD = HG_HEADS * HG_DK
HG_DV = D_MODEL // HG_HEADS
HG_CHUNK = 64

ATT_HEADS = 16
ATT_KV_HEADS = 2
ATT_GROUP = ATT_HEADS // ATT_KV_HEADS
ATT_HEAD_DIM = 64
WINDOW = 128
ATT_SCALE = ATT_HEAD_DIM ** -0.5
ROPE_THETA = 500000.0
ROT_DIM = ATT_HEAD_DIM // 4
EPS = 1e-6

LANES = 128
HEADS_PER_SLAB = LANES // ATT_HEAD_DIM
Q_SLABS = ATT_HEADS // HEADS_PER_SLAB
SLABS_PER_KV = Q_SLABS // ATT_KV_HEADS
MASKED = -1e30

FFN_TM = 512
FFN_CF = 256
HG_TS = 256
KV_TM = 1024
ATT_TQ = 512
VMEM_LIMIT = 56 * 1024 * 1024

BF16 = jnp.bfloat16
F32 = jnp.float32


def _dot(a, b):
    return jnp.dot(a, b, preferred_element_type=F32)


def _dot_nt(a, b):
    return lax.dot_general(a, b, (((1,), (1,)), ((), ())), preferred_element_type=F32)


def _dot_tn(a, b):
    return lax.dot_general(a, b, (((0,), (0,)), ((), ())), preferred_element_type=F32)


def _rms(x, g):
    ms = jnp.mean(x * x, axis=-1, keepdims=True)
    return x * lax.rsqrt(ms + EPS) * g


def _sigmoid(x):
    return 1.0 / (1.0 + jnp.exp(-x))


def _group_mean_sq(x, group_ones, inv_n):
    x2 = x * x
    hi = x2.astype(BF16)
    lo = (x2 - hi.astype(F32)).astype(BF16)
    return (_dot(hi, group_ones) + _dot(lo, group_ones)) * inv_n


def _group_ones(n, group):
    r = lax.broadcasted_iota(jnp.int32, (n, n), 0) // group
    c = lax.broadcasted_iota(jnp.int32, (n, n), 1) // group
    return jnp.where(r == c, 1.0, 0.0).astype(BF16)


def _rope_slab(x, cos_t, sin_lo, sin_hi):
    return (x * cos_t + pltpu.roll(x, LANES - ROT_DIM // 2, axis=1) * sin_lo
            + pltpu.roll(x, ROT_DIM // 2, axis=1) * sin_hi)


def _resident(shape):
    nd = len(shape)
    return pl.BlockSpec(shape, lambda *_: (0,) * nd, pipeline_mode=pl.Buffered(1))


def _ffn_kernel(x_ref, g_ref, wgu_ref, wd_ref, o_ref):
    x = x_ref[...]
    hn = _rms(x, g_ref[...]).astype(BF16)
    acc = None
    for c in range(D_FF // FFN_CF):
        lo = c * FFN_CF
        gate = _dot(hn, wgu_ref[:, lo:lo + FFN_CF])
        up = _dot(hn, wgu_ref[:, D_FF + lo:D_FF + lo + FFN_CF])
        act = (gate * _sigmoid(gate) * up).astype(BF16)
        part = _dot(act, wd_ref[lo:lo + FFN_CF, :])
        acc = part if acc is None else acc + part
    o_ref[...] = x + MACARON_WEIGHT * acc


def _ffn(h, g, w_gu, w_d, layer, idx):
    t = h.shape[0]
    return pl.pallas_call(
        _ffn_kernel,
        out_shape=jax.ShapeDtypeStruct((t, D_MODEL), F32),
        grid=(t // FFN_TM,),
        in_specs=[
            pl.BlockSpec((FFN_TM, D_MODEL), lambda i: (i, 0)),
            pl.BlockSpec((None, None, 1, D_MODEL), lambda i: (layer, idx, 0, 0)),
            pl.BlockSpec((None, None, D_MODEL, 2 * D_FF), lambda i: (layer, idx, 0, 0),
                         pipeline_mode=pl.Buffered(1)),
            pl.BlockSpec((None, None, D_FF, D_MODEL), lambda i: (layer, idx, 0, 0),
                         pipeline_mode=pl.Buffered(1)),
        ],
        out_specs=pl.BlockSpec((FFN_TM, D_MODEL), lambda i: (i, 0)),
        compiler_params=pltpu.CompilerParams(
            dimension_semantics=("parallel",), vmem_limit_bytes=VMEM_LIMIT),
        name=f"ffn_{layer}_{idx}",
    )(h, g, w_gu, w_d)


def _hgrn_kernel(h_ref, g_ref, win_ref, lbl_ref, og_ref, wout_ref, o_ref,
                 proj_ref, obuf_ref, state_ref):
    @pl.when(pl.program_id(1) == 0)
    def _():
        state_ref[...] = jnp.zeros_like(state_ref)

    x = h_ref[...]
    hn = _rms(x, g_ref[...]).astype(BF16)
    proj_ref[...] = _dot(hn, win_ref[...])

    logits = lbl_ref[...]
    top = jnp.max(logits, axis=0, keepdims=True)
    e = jnp.exp(logits - top)
    lower = e[0:1, :] / jnp.sum(e, axis=0, keepdims=True)

    row = lax.broadcasted_iota(jnp.int32, (HG_CHUNK, HG_CHUNK), 0)
    col = lax.broadcasted_iota(jnp.int32, (HG_CHUNK, HG_CHUNK), 1)
    causal = col <= row
    tril = jnp.where(causal, 1.0, 0.0).astype(F32)

    def chunk(c, carry):
        r0 = pl.multiple_of(c * HG_CHUNK, HG_CHUNK)
        rows = pl.ds(r0, HG_CHUNK)
        q = proj_ref[rows, 0:HG_FD]
        q = q * _sigmoid(q)
        f = proj_ref[rows, HG_FD:2 * HG_FD]
        forget = lower + (1.0 - lower) * _sigmoid(f)
        k = 1.0 - forget
        log_f = jnp.log(forget)
        b = jnp.dot(tril, log_f, preferred_element_type=F32, precision=lax.Precision.HIGHEST)
        b_end = b[HG_CHUNK - 1:HG_CHUNK, :]
        q_e = (q * jnp.exp(b)).astype(BF16)
        k_e = (k * jnp.exp(-b)).astype(BF16)
        k_d = (k * jnp.exp(b_end - b)).astype(BF16)
        decay = jnp.exp(b_end)
        v = proj_ref[rows, 2 * HG_FD:2 * HG_FD + D_MODEL].astype(BF16)
        for hd in range(HG_HEADS):
            ks = slice(hd * HG_DK, (hd + 1) * HG_DK)
            vs = slice(hd * HG_DV, (hd + 1) * HG_DV)
            scores = jnp.where(causal, _dot_nt(q_e[:, ks], k_e[:, ks]), 0.0)
            st = state_ref[hd]
            o = _dot_nt(q_e[:, ks], st.astype(BF16)) + _dot(scores.astype(BF16), v[:, vs])
            obuf_ref[rows, vs] = o
            state_ref[hd] = decay[:, ks] * st + _dot_tn(v[:, vs], k_d[:, ks])
        return carry

    lax.fori_loop(0, HG_TS // HG_CHUNK, chunk, 0)

    og = og_ref[...]
    gate = proj_ref[:, 2 * HG_FD + D_MODEL:]
    gate = gate * _sigmoid(gate)
    parts = []
    for hd in range(HG_HEADS):
        vs = slice(hd * HG_DV, (hd + 1) * HG_DV)
        parts.append(_rms(obuf_ref[:, vs], og) * gate[:, vs])
    y = jnp.concatenate(parts, axis=1).astype(BF16)
    o_ref[...] = x + _dot(y, wout_ref[...])


def _hgrn(h, g, w_in, lb_logits, onorm_g, w_out, batch, seq):
    steps = seq // HG_TS
    n_lb = lb_logits.shape[0]
    return pl.pallas_call(
        _hgrn_kernel,
        out_shape=jax.ShapeDtypeStruct(h.shape, F32),
        grid=(batch, steps),
        in_specs=[
            pl.BlockSpec((HG_TS, D_MODEL), lambda b, s: (b * steps + s, 0)),
            _resident((1, D_MODEL)),
            _resident((D_MODEL, 2 * HG_FD + 2 * D_MODEL)),
            _resident((n_lb, HG_FD)),
            _resident((1, HG_DV)),
            _resident((D_MODEL, D_MODEL)),
        ],
        out_specs=pl.BlockSpec((HG_TS, D_MODEL), lambda b, s: (b * steps + s, 0)),
        scratch_shapes=[
            pltpu.VMEM((HG_TS, 2 * HG_FD + 2 * D_MODEL), F32),
            pltpu.VMEM((HG_TS, D_MODEL), F32),
            pltpu.VMEM((HG_HEADS, HG_DV, HG_DK), F32),
        ],
        compiler_params=pltpu.CompilerParams(
            dimension_semantics=("parallel", "arbitrary"), vmem_limit_bytes=VMEM_LIMIT),
        name="hgrn",
    )(h, g, w_in, lb_logits, onorm_g, w_out)


def _kv_kernel(h_ref, g_ref, w_ref, kg_ref, cos_ref, slo_ref, shi_ref, o_ref):
    hn = _rms(h_ref[...], g_ref[...]).astype(BF16)
    kv = _dot(hn, w_ref[...])
    k = kv[:, :LANES]
    v = kv[:, LANES:]
    ms = _group_mean_sq(k, _group_ones(LANES, ATT_HEAD_DIM), 1.0 / ATT_HEAD_DIM)
    k = k * lax.rsqrt(ms + EPS) * kg_ref[...]
    k = _rope_slab(k, cos_ref[...], slo_ref[...], shi_ref[...])
    low = lax.broadcasted_iota(jnp.int32, k.shape, 1) < ATT_HEAD_DIM
    for base, t in ((0, k), (4 * LANES, v)):
        swapped = pltpu.roll(t, ATT_HEAD_DIM, axis=1)
        o_ref[:, base + 0 * LANES:base + 1 * LANES] = jnp.where(low, t, 0.0).astype(BF16)
        o_ref[:, base + 1 * LANES:base + 2 * LANES] = jnp.where(low, 0.0, swapped).astype(BF16)
        o_ref[:, base + 2 * LANES:base + 3 * LANES] = jnp.where(low, swapped, 0.0).astype(BF16)
        o_ref[:, base + 3 * LANES:base + 4 * LANES] = jnp.where(low, 0.0, t).astype(BF16)


def _shared_kv(h, g, w, kg, cos_t, sin_lo, sin_hi, seq):
    t = h.shape[0]
    per_seq = seq // KV_TM
    return pl.pallas_call(
        _kv_kernel,
        out_shape=jax.ShapeDtypeStruct((t, 8 * LANES), BF16),
        grid=(t // KV_TM,),
        in_specs=[
            pl.BlockSpec((KV_TM, D_MODEL), lambda i: (i, 0)),
            _resident((1, D_MODEL)),
            _resident((D_MODEL, 2 * LANES)),
            _resident((1, LANES)),
            pl.BlockSpec((KV_TM, LANES), lambda i: (i % per_seq, 0)),
            pl.BlockSpec((KV_TM, LANES), lambda i: (i % per_seq, 0)),
            pl.BlockSpec((KV_TM, LANES), lambda i: (i % per_seq, 0)),
        ],
        out_specs=pl.BlockSpec((KV_TM, 8 * LANES), lambda i: (i, 0)),
        compiler_params=pltpu.CompilerParams(
            dimension_semantics=("parallel",), vmem_limit_bytes=VMEM_LIMIT),
        name="shared_kv",
    )(h, g, w, kg, cos_t, sin_lo, sin_hi)


def _swa_kernel(sink_ref, h_ref, g_ref, wq_ref, qg_ref, cos_ref, slo_ref, shi_ref,
                halo_ref, kv_ref, wo_ref, o_ref, q_buf, kv_buf, o_buf):
    step = pl.program_id(1)
    x = h_ref[...]
    hn = _rms(x, g_ref[...]).astype(BF16)
    q = _dot(hn, wq_ref[...])
    ones = _group_ones(LANES, ATT_HEAD_DIM)
    qg = qg_ref[...]
    cos_t = cos_ref[...]
    sin_lo = slo_ref[...]
    sin_hi = shi_ref[...]
    for s in range(Q_SLABS):
        qs = q[:, s * LANES:(s + 1) * LANES]
        ms = _group_mean_sq(qs, ones, 1.0 / ATT_HEAD_DIM)
        qs = _rope_slab(qs * lax.rsqrt(ms + EPS) * qg, cos_t, sin_lo, sin_hi)
        q_buf[:, s * LANES:(s + 1) * LANES] = (qs * ATT_SCALE).astype(BF16)

    kv_buf[0:WINDOW, :] = halo_ref[...]
    kv_buf[WINDOW:, :] = kv_ref[...]

    qi = lax.broadcasted_iota(jnp.int32, (WINDOW, 2 * WINDOW), 0)
    kj = lax.broadcasted_iota(jnp.int32, (WINDOW, 2 * WINDOW), 1)
    in_window = (kj > qi) & (kj <= qi + WINDOW)

    def block(qb, carry):
        r0 = pl.multiple_of(qb * WINDOW, WINDOW)
        rows = pl.ds(r0, WINDOW)
        band = pl.ds(r0, 2 * WINDOW)
        first = jnp.logical_and(step == 0, qb == 0)
        allowed = in_window & (kj >= jnp.where(first, WINDOW, 0))
        for kvh in range(ATT_KV_HEADS):
            q_st = jnp.concatenate(
                [q_buf[rows, (kvh * SLABS_PER_KV + p) * LANES:(kvh * SLABS_PER_KV + p + 1) * LANES]
                 for p in range(SLABS_PER_KV)], axis=0)
            pv = []
            rinv = []
            for par in range(HEADS_PER_SLAB):
                kslab = (kvh * HEADS_PER_SLAB + par) * LANES
                vslab = (4 + kvh * HEADS_PER_SLAB + par) * LANES
                s_all = _dot_nt(q_st, kv_buf[band, kslab:kslab + LANES])
                es = []
                rs = []
                for p in range(SLABS_PER_KV):
                    head = kvh * ATT_GROUP + p * HEADS_PER_SLAB + par
                    sink = sink_ref[head]
                    sc = jnp.where(allowed, s_all[p * WINDOW:(p + 1) * WINDOW, :], MASKED)
                    m = jnp.maximum(jnp.max(sc, axis=1, keepdims=True), sink)
                    ex = jnp.exp(sc - m)
                    denom = jnp.sum(ex, axis=1, keepdims=True) + jnp.exp(sink - m)
                    es.append(ex.astype(BF16))
                    rs.append(1.0 / denom)
                e_all = jnp.concatenate(es, axis=0)
                pv.append(_dot(e_all, kv_buf[band, vslab:vslab + LANES]))
                rinv.append(rs)
            for p in range(SLABS_PER_KV):
                pr = slice(p * WINDOW, (p + 1) * WINDOW)
                o = pv[0][pr, :] * rinv[0][p] + pv[1][pr, :] * rinv[1][p]
                sl = (kvh * SLABS_PER_KV + p) * LANES
                o_buf[rows, sl:sl + LANES] = o.astype(BF16)
        return carry

    lax.fori_loop(0, ATT_TQ // WINDOW, block, 0)
    o_ref[...] = x + _dot(o_buf[...], wo_ref[...])


def _swa(h, g, w_q, qg, sinks, cos_t, sin_lo, sin_hi, kv, w_o, batch, seq):
    steps = seq // ATT_TQ
    blocks_per_step = ATT_TQ // WINDOW
    blocks_per_seq = seq // WINDOW

    def halo_map(b, s):
        return (jnp.maximum(b * blocks_per_seq + s * blocks_per_step - 1, 0), 0)

    return pl.pallas_call(
        _swa_kernel,
        out_shape=jax.ShapeDtypeStruct(h.shape, F32),
        grid=(batch, steps),
        in_specs=[
            pl.BlockSpec(memory_space=pltpu.SMEM),
            pl.BlockSpec((ATT_TQ, D_MODEL), lambda b, s: (b * steps + s, 0)),
            _resident((1, D_MODEL)),
            _resident((D_MODEL, D_MODEL)),
            _resident((1, LANES)),
            pl.BlockSpec((ATT_TQ, LANES), lambda b, s: (s, 0)),
            pl.BlockSpec((ATT_TQ, LANES), lambda b, s: (s, 0)),
            pl.BlockSpec((ATT_TQ, LANES), lambda b, s: (s, 0)),
            pl.BlockSpec((WINDOW, 8 * LANES), halo_map),
            pl.BlockSpec((ATT_TQ, 8 * LANES), lambda b, s: (b * steps + s, 0)),
            _resident((D_MODEL, D_MODEL)),
        ],
        out_specs=pl.BlockSpec((ATT_TQ, D_MODEL), lambda b, s: (b * steps + s, 0)),
        scratch_shapes=[
            pltpu.VMEM((ATT_TQ, D_MODEL), BF16),
            pltpu.VMEM((ATT_TQ + WINDOW, 8 * LANES), BF16),
            pltpu.VMEM((ATT_TQ, D_MODEL), BF16),
        ],
        compiler_params=pltpu.CompilerParams(
            dimension_semantics=("parallel", "parallel"), vmem_limit_bytes=VMEM_LIMIT),
        name="swa",
    )(sinks, h, g, w_q, qg, cos_t, sin_lo, sin_hi, kv, kv, w_o)


def _rope_tables(seq):
    half = ROT_DIM // 2
    inv_freq = jnp.power(ROPE_THETA, -jnp.arange(half, dtype=F32) * (2.0 / ROT_DIM))
    ang = jnp.arange(seq).astype(F32)[:, None] * inv_freq[None, :]
    cos, sin = jnp.cos(ang), jnp.sin(ang)
    zeros = jnp.zeros_like(sin)
    rest = ATT_HEAD_DIM - ROT_DIM
    pad0 = jnp.zeros((seq, rest), F32)
    cos_t = jnp.concatenate([cos, cos, jnp.ones((seq, rest), F32)], axis=1)
    sin_lo = jnp.concatenate([-sin, zeros, pad0], axis=1)
    sin_hi = jnp.concatenate([zeros, sin, pad0], axis=1)
    return tuple(jnp.tile(t, (1, HEADS_PER_SLAB)) for t in (cos_t, sin_lo, sin_hi))


def kernel(x, ffn_norm_g, ffn_w_gate_up, ffn_w_down, mix_norm_g, hgrn_w_in, hgrn_lb_logits,
           hgrn_onorm_g, hgrn_w_out, kv_norm_g, kv_w, k_norm_g, attn_w_q, q_norm_g,
           attn_sinks, attn_w_out):
    batch, seq, d = x.shape
    assert d == D_MODEL and seq % ATT_TQ == 0 and seq % HG_TS == 0 and seq % KV_TM == 0
    assert (batch * seq) % FFN_TM == 0
    h = x.reshape(batch * seq, d)

    ffn_g = ffn_norm_g.reshape(ffn_norm_g.shape[0], 2, 1, D_MODEL)
    w_gu = ffn_w_gate_up.astype(BF16)
    w_d = ffn_w_down.astype(BF16)
    cos_t, sin_lo, sin_hi = _rope_tables(seq)

    h = _ffn(h, ffn_g, w_gu, w_d, 0, 0)
    h = _hgrn(h, mix_norm_g[0].reshape(1, D_MODEL), hgrn_w_in[0].astype(BF16),
              hgrn_lb_logits, hgrn_onorm_g[0].reshape(1, HG_DV),
              hgrn_w_out[0].astype(BF16), batch, seq)
    h = _ffn(h, ffn_g, w_gu, w_d, 0, 1)
    kv = _shared_kv(h, kv_norm_g.reshape(1, D_MODEL), kv_w.astype(BF16),
                    jnp.tile(k_norm_g, HEADS_PER_SLAB).reshape(1, LANES),
                    cos_t, sin_lo, sin_hi, seq)
    h = _ffn(h, ffn_g, w_gu, w_d, 1, 0)
    h = _swa(h, mix_norm_g[1].reshape(1, D_MODEL), attn_w_q[0].astype(BF16),
             jnp.tile(q_norm_g[0], HEADS_PER_SLAB).reshape(1, LANES), attn_sinks[0],
             cos_t, sin_lo, sin_hi, kv, attn_w_out[0].astype(BF16), batch, seq)
    h = _ffn(h, ffn_g, w_gu, w_d, 1, 1)
    return h.reshape(batch, seq, d)
```

```python
import functools

import jax
import jax.numpy as jnp
from jax import lax
from jax.experimental import pallas as pl
from jax.experimental.pallas import tpu as pltpu

D_MODEL = 1024
D_FF = 2816
MACARON_WEIGHT = 0.5

HG_HEADS = 8
HG_DK = 128
HG_FD = HG_HEADS * HG_DK
HG_DV = D_MODEL // HG_HEADS
HG_CHUNK = 64

ATT_HEADS = 16
ATT_KV_HEADS = 2
ATT_GROUP = ATT_HEADS // ATT_KV_HEADS
ATT_HEAD_DIM = 64
WINDOW = 128
ATT_SCALE = ATT_HEAD_DIM ** -0.5
ROPE_THETA = 500000.0
ROT_DIM = ATT_HEAD_DIM // 4
EPS = 1e-6

LANES = 128
HEADS_PER_SLAB = LANES // ATT_HEAD_DIM
Q_SLABS = ATT_HEADS // HEADS_PER_SLAB
SLABS_PER_KV = Q_SLABS // ATT_KV_HEADS
MASKED = -1e30

FFN_TM = 512
FFN_CF = 256
HG_TS = 256
KV_TM = 1024
ATT_TQ = 512
VMEM_LIMIT = 56 * 1024 * 1024

BF16 = jnp.bfloat16
F32 = jnp.float32


def _dot(a, b):
    return jnp.dot(a, b, preferred_element_type=F32)


def _dot_nt(a, b):
    return lax.dot_general(a, b, (((1,), (1,)), ((), ())), preferred_element_type=F32)


def _dot_tn(a, b):
    return lax.dot_general(a, b, (((0,), (0,)), ((), ())), preferred_element_type=F32)


def _rms(x, g):
    ms = jnp.mean(x * x, axis=-1, keepdims=True)
    return x * lax.rsqrt(ms + EPS) * g


def _sigmoid(x):
    return 1.0 / (1.0 + jnp.exp(-x))


def _group_mean_sq(x, group_ones, inv_n):
    x2 = x * x
    hi = x2.astype(BF16)
    lo = (x2 - hi.astype(F32)).astype(BF16)
    return (_dot(hi, group_ones) + _dot(lo, group_ones)) * inv_n


def _group_ones(n, group):
    r = lax.broadcasted_iota(jnp.int32, (n, n), 0) // group
    c = lax.broadcasted_iota(jnp.int32, (n, n), 1) // group
    return jnp.where(r == c, 1.0, 0.0).astype(BF16)


def _rope_slab(x, cos_t, sin_lo, sin_hi):
    return (x * cos_t + pltpu.roll(x, LANES - ROT_DIM // 2, axis=1) * sin_lo
            + pltpu.roll(x, ROT_DIM // 2, axis=1) * sin_hi)


def _resident(shape):
    nd = len(shape)
    return pl.BlockSpec(shape, lambda *_: (0,) * nd, pipeline_mode=pl.Buffered(1))


def _ffn_kernel(x_ref, g_ref, wgu_ref, wd_ref, o_ref):
    x = x_ref[...]
    hn = _rms(x, g_ref[...]).astype(BF16)
    acc = None
    for c in range(D_FF // FFN_CF):
        lo = c * FFN_CF
        gate = _dot(hn, wgu_ref[:, lo:lo + FFN_CF])
        up = _dot(hn, wgu_ref[:, D_FF + lo:D_FF + lo + FFN_CF])
        act = (gate * _sigmoid(gate) * up).astype(BF16)
        part = _dot(act, wd_ref[lo:lo + FFN_CF, :])
        acc = part if acc is None else acc + part
    o_ref[...] = x + MACARON_WEIGHT * acc


def _ffn(h, g, w_gu, w_d, layer, idx):
    t = h.shape[0]
    return pl.pallas_call(
        _ffn_kernel,
        out_shape=jax.ShapeDtypeStruct((t, D_MODEL), F32),
        grid=(t // FFN_TM,),
        in_specs=[
            pl.BlockSpec((FFN_TM, D_MODEL), lambda i: (i, 0)),
            pl.BlockSpec((None, None, 1, D_MODEL), lambda i: (layer, idx, 0, 0)),
            pl.BlockSpec((None, None, D_MODEL, 2 * D_FF), lambda i: (layer, idx, 0, 0),
                         pipeline_mode=pl.Buffered(1)),
            pl.BlockSpec((None, None, D_FF, D_MODEL), lambda i: (layer, idx, 0, 0),
                         pipeline_mode=pl.Buffered(1)),
        ],
        out_specs=pl.BlockSpec((FFN_TM, D_MODEL), lambda i: (i, 0)),
        compiler_params=pltpu.CompilerParams(
            dimension_semantics=("parallel",), vmem_limit_bytes=VMEM_LIMIT),
        name=f"ffn_{layer}_{idx}",
    )(h, g, w_gu, w_d)


def _hgrn_kernel(h_ref, g_ref, win_ref, lbl_ref, og_ref, wout_ref, o_ref,
                 qe_ref, ke_ref, kd_ref, v_ref, gate_ref, y_ref, state_ref):
    n_chunks = HG_TS // HG_CHUNK

    @pl.when(pl.program_id(1) == 0)
    def _():
        state_ref[...] = jnp.zeros_like(state_ref)

    x = h_ref[...]
    hn = _rms(x, g_ref[...]).astype(BF16)

    logits = lbl_ref[...]
    top = jnp.max(logits, axis=0, keepdims=True)
    e = jnp.exp(logits - top)
    lower = e[0:1, :] / jnp.sum(e, axis=0, keepdims=True)

    row = lax.broadcasted_iota(jnp.int32, (HG_TS, HG_TS), 0)
    col = lax.broadcasted_iota(jnp.int32, (HG_TS, HG_TS), 1)
    causal = (col <= row) & (col // HG_CHUNK == row // HG_CHUNK)
    tril = jnp.where(causal, 1.0, 0.0).astype(BF16)

    forget = lower + (1.0 - lower) * _sigmoid(_dot(hn, win_ref[:, HG_FD:2 * HG_FD]))
    k = 1.0 - forget
    log_f = jnp.log(forget)
    hi = log_f.astype(BF16)
    lo = (log_f - hi.astype(F32)).astype(BF16)
    b = _dot(tril, hi) + _dot(tril, lo)
    q = _dot(hn, win_ref[:, 0:HG_FD])
    qe_ref[...] = (q * _sigmoid(q) * jnp.exp(b)).astype(BF16)
    ke_ref[...] = (k * jnp.exp(-b)).astype(BF16)
    decay = []
    for c in range(n_chunks):
        rs = slice(c * HG_CHUNK, (c + 1) * HG_CHUNK)
        b_end = b[(c + 1) * HG_CHUNK - 1:(c + 1) * HG_CHUNK, :]
        kd_ref[rs, :] = (k[rs, :] * jnp.exp(b_end - b[rs, :])).astype(BF16)
        decay.append(jnp.exp(b_end))
    v_ref[...] = _dot(hn, win_ref[:, 2 * HG_FD:2 * HG_FD + D_MODEL]).astype(BF16)
    gate = _dot(hn, win_ref[:, 2 * HG_FD + D_MODEL:])
    gate_ref[...] = gate * _sigmoid(gate)

    og = og_ref[...]
    for hd in range(HG_HEADS):
        ks = slice(hd * HG_DK, (hd + 1) * HG_DK)
        vs = slice(hd * HG_DV, (hd + 1) * HG_DV)
        q_e = qe_ref[:, ks]
        scores = jnp.where(causal, _dot_nt(q_e, ke_ref[:, ks]), 0.0).astype(BF16)
        o_intra = _dot(scores, v_ref[:, vs])
        st = state_ref[hd]
        o_parts = []
        for c in range(n_chunks):
            rs = slice(c * HG_CHUNK, (c + 1) * HG_CHUNK)
            o_parts.append(o_intra[rs, :] + _dot_nt(q_e[rs, :], st.astype(BF16)))
            st = decay[c][:, ks] * st + _dot_tn(v_ref[rs, vs], kd_ref[rs, ks])
        state_ref[hd] = st
        o = jnp.concatenate(o_parts, axis=0)
        y_ref[:, vs] = (_rms(o, og) * gate_ref[:, vs]).astype(BF16)

    o_ref[...] = x + _dot(y_ref[...], wout_ref[...])


def _hgrn(h, g, w_in, lb_logits, onorm_g, w_out, batch, seq):
    steps = seq // HG_TS
    n_lb = lb_logits.shape[0]
    return pl.pallas_call(
        _hgrn_kernel,
        out_shape=jax.ShapeDtypeStruct(h.shape, F32),
        grid=(batch, steps),
        in_specs=[
            pl.BlockSpec((HG_TS, D_MODEL), lambda b, s: (b * steps + s, 0)),
            _resident((1, D_MODEL)),
            _resident((D_MODEL, 2 * HG_FD + 2 * D_MODEL)),
            _resident((n_lb, HG_FD)),
            _resident((1, HG_DV)),
            _resident((D_MODEL, D_MODEL)),
        ],
        out_specs=pl.BlockSpec((HG_TS, D_MODEL), lambda b, s: (b * steps + s, 0)),
        scratch_shapes=[
            pltpu.VMEM((HG_TS, HG_FD), BF16),
            pltpu.VMEM((HG_TS, HG_FD), BF16),
            pltpu.VMEM((HG_TS, HG_FD), BF16),
            pltpu.VMEM((HG_TS, D_MODEL), BF16),
            pltpu.VMEM((HG_TS, D_MODEL), F32),
            pltpu.VMEM((HG_TS, D_MODEL), BF16),
            pltpu.VMEM((HG_HEADS, HG_DV, HG_DK), F32),
        ],
        compiler_params=pltpu.CompilerParams(
            dimension_semantics=("parallel", "arbitrary"), vmem_limit_bytes=VMEM_LIMIT),
        name="hgrn",
    )(h, g, w_in, lb_logits, onorm_g, w_out)


def _kv_kernel(h_ref, g_ref, w_ref, kg_ref, cos_ref, slo_ref, shi_ref, o_ref):
    hn = _rms(h_ref[...], g_ref[...]).astype(BF16)
    kv = _dot(hn, w_ref[...])
    k = kv[:, :LANES]
    v = kv[:, LANES:]
    ms = _group_mean_sq(k, _group_ones(LANES, ATT_HEAD_DIM), 1.0 / ATT_HEAD_DIM)
    k = k * lax.rsqrt(ms + EPS) * kg_ref[...]
    k = _rope_slab(k, cos_ref[...], slo_ref[...], shi_ref[...])
    low = lax.broadcasted_iota(jnp.int32, k.shape, 1) < ATT_HEAD_DIM
    for base, t in ((0, k), (4 * LANES, v)):
        swapped = pltpu.roll(t, ATT_HEAD_DIM, axis=1)
        o_ref[:, base + 0 * LANES:base + 1 * LANES] = jnp.where(low, t, 0.0).astype(BF16)
        o_ref[:, base + 1 * LANES:base + 2 * LANES] = jnp.where(low, 0.0, swapped).astype(BF16)
        o_ref[:, base + 2 * LANES:base + 3 * LANES] = jnp.where(low, swapped, 0.0).astype(BF16)
        o_ref[:, base + 3 * LANES:base + 4 * LANES] = jnp.where(low, 0.0, t).astype(BF16)


def _shared_kv(h, g, w, kg, cos_t, sin_lo, sin_hi, seq):
    t = h.shape[0]
    per_seq = seq // KV_TM
    return pl.pallas_call(
        _kv_kernel,
        out_shape=jax.ShapeDtypeStruct((t, 8 * LANES), BF16),
        grid=(t // KV_TM,),
        in_specs=[
            pl.BlockSpec((KV_TM, D_MODEL), lambda i: (i, 0)),
            _resident((1, D_MODEL)),
            _resident((D_MODEL, 2 * LANES)),
            _resident((1, LANES)),
            pl.BlockSpec((KV_TM, LANES), lambda i: (i % per_seq, 0)),
            pl.BlockSpec((KV_TM, LANES), lambda i: (i % per_seq, 0)),
            pl.BlockSpec((KV_TM, LANES), lambda i: (i % per_seq, 0)),
        ],
        out_specs=pl.BlockSpec((KV_TM, 8 * LANES), lambda i: (i, 0)),
        compiler_params=pltpu.CompilerParams(
            dimension_semantics=("parallel",), vmem_limit_bytes=VMEM_LIMIT),
        name="shared_kv",
    )(h, g, w, kg, cos_t, sin_lo, sin_hi)


def _swa_kernel(sink_ref, h_ref, g_ref, wq_ref, qg_ref, cos_ref, slo_ref, shi_ref,
                halo_ref, kv_ref, wo_ref, o_ref, q_buf, kv_buf, o_buf):
    step = pl.program_id(1)
    x = h_ref[...]
    hn = _rms(x, g_ref[...]).astype(BF16)
    q = _dot(hn, wq_ref[...])
    ones = _group_ones(LANES, ATT_HEAD_DIM)
    qg = qg_ref[...]
    cos_t = cos_ref[...]
    sin_lo = slo_ref[...]
    sin_hi = shi_ref[...]
    for s in range(Q_SLABS):
        qs = q[:, s * LANES:(s + 1) * LANES]
        ms = _group_mean_sq(qs, ones, 1.0 / ATT_HEAD_DIM)
        qs = _rope_slab(qs * lax.rsqrt(ms + EPS) * qg, cos_t, sin_lo, sin_hi)
        q_buf[:, s * LANES:(s + 1) * LANES] = (qs * ATT_SCALE).astype(BF16)

    kv_buf[0:WINDOW, :] = halo_ref[...]
    kv_buf[WINDOW:, :] = kv_ref[...]

    qi = lax.broadcasted_iota(jnp.int32, (WINDOW, 2 * WINDOW), 0)
    kj = lax.broadcasted_iota(jnp.int32, (WINDOW, 2 * WINDOW), 1)
    in_window = (kj > qi) & (kj <= qi + WINDOW)

    def block(qb, carry):
        r0 = pl.multiple_of(qb * WINDOW, WINDOW)
        rows = pl.ds(r0, WINDOW)
        band = pl.ds(r0, 2 * WINDOW)
        first = jnp.logical_and(step == 0, qb == 0)
        allowed = in_window & (kj >= jnp.where(first, WINDOW, 0))
        for kvh in range(ATT_KV_HEADS):
            q_st = jnp.concatenate(
                [q_buf[rows, (kvh * SLABS_PER_KV + p) * LANES:(kvh * SLABS_PER_KV + p + 1) * LANES]
                 for p in range(SLABS_PER_KV)], axis=0)
            pv = []
            rinv = []
            for par in range(HEADS_PER_SLAB):
                kslab = (kvh * HEADS_PER_SLAB + par) * LANES
                vslab = (4 + kvh * HEADS_PER_SLAB + par) * LANES
                s_all = _dot_nt(q_st, kv_buf[band, kslab:kslab + LANES])
                es = []
                rs = []
                for p in range(SLABS_PER_KV):
                    head = kvh * ATT_GROUP + p * HEADS_PER_SLAB + par
                    sink = sink_ref[head]
                    sc = jnp.where(allowed, s_all[p * WINDOW:(p + 1) * WINDOW, :], MASKED)
                    m = jnp.maximum(jnp.max(sc, axis=1, keepdims=True), sink)
                    ex = jnp.exp(sc - m)
                    denom = jnp.sum(ex, axis=1, keepdims=True) + jnp.exp(sink - m)
                    es.append(ex.astype(BF16))
                    rs.append(1.0 / denom)
                e_all = jnp.concatenate(es, axis=0)
                pv.append(_dot(e_all, kv_buf[band, vslab:vslab + LANES]))
                rinv.append(rs)
            for p in range(SLABS_PER_KV):
                pr = slice(p * WINDOW, (p + 1) * WINDOW)
                o = pv[0][pr, :] * rinv[0][p] + pv[1][pr, :] * rinv[1][p]
                sl = (kvh * SLABS_PER_KV + p) * LANES
                o_buf[rows, sl:sl + LANES] = o.astype(BF16)
        return carry

    lax.fori_loop(0, ATT_TQ // WINDOW, block, 0)
    o_ref[...] = x + _dot(o_buf[...], wo_ref[...])


def _swa(h, g, w_q, qg, sinks, cos_t, sin_lo, sin_hi, kv, w_o, batch, seq):
    steps = seq // ATT_TQ
    blocks_per_step = ATT_TQ // WINDOW
    blocks_per_seq = seq // WINDOW

    def halo_map(b, s):
        return (jnp.maximum(b * blocks_per_seq + s * blocks_per_step - 1, 0), 0)

    return pl.pallas_call(
        _swa_kernel,
        out_shape=jax.ShapeDtypeStruct(h.shape, F32),
        grid=(batch, steps),
        in_specs=[
            pl.BlockSpec(memory_space=pltpu.SMEM),
            pl.BlockSpec((ATT_TQ, D_MODEL), lambda b, s: (b * steps + s, 0)),
            _resident((1, D_MODEL)),
            _resident((D_MODEL, D_MODEL)),
            _resident((1, LANES)),
            pl.BlockSpec((ATT_TQ, LANES), lambda b, s: (s, 0)),
            pl.BlockSpec((ATT_TQ, LANES), lambda b, s: (s, 0)),
            pl.BlockSpec((ATT_TQ, LANES), lambda b, s: (s, 0)),
            pl.BlockSpec((WINDOW, 8 * LANES), halo_map),
            pl.BlockSpec((ATT_TQ, 8 * LANES), lambda b, s: (b * steps + s, 0)),
            _resident((D_MODEL, D_MODEL)),
        ],
        out_specs=pl.BlockSpec((ATT_TQ, D_MODEL), lambda b, s: (b * steps + s, 0)),
        scratch_shapes=[
            pltpu.VMEM((ATT_TQ, D_MODEL), BF16),
            pltpu.VMEM((ATT_TQ + WINDOW, 8 * LANES), BF16),
            pltpu.VMEM((ATT_TQ, D_MODEL), BF16),
        ],
        compiler_params=pltpu.CompilerParams(
            dimension_semantics=("parallel", "parallel"), vmem_limit_bytes=VMEM_LIMIT),
        name="swa",
    )(sinks, h, g, w_q, qg, cos_t, sin_lo, sin_hi, kv, kv, w_o)


def _rope_tables(seq):
    half = ROT_DIM // 2
    inv_freq = jnp.power(ROPE_THETA, -jnp.arange(half, dtype=F32) * (2.0 / ROT_DIM))
    ang = jnp.arange(seq).astype(F32)[:, None] * inv_freq[None, :]
    cos, sin = jnp.cos(ang), jnp.sin(ang)
    zeros = jnp.zeros_like(sin)
    rest = ATT_HEAD_DIM - ROT_DIM
    pad0 = jnp.zeros((seq, rest), F32)
    cos_t = jnp.concatenate([cos, cos, jnp.ones((seq, rest), F32)], axis=1)
    sin_lo = jnp.concatenate([-sin, zeros, pad0], axis=1)
    sin_hi = jnp.concatenate([zeros, sin, pad0], axis=1)
    return tuple(jnp.tile(t, (1, HEADS_PER_SLAB)) for t in (cos_t, sin_lo, sin_hi))


def kernel(x, ffn_norm_g, ffn_w_gate_up, ffn_w_down, mix_norm_g, hgrn_w_in, hgrn_lb_logits,
           hgrn_onorm_g, hgrn_w_out, kv_norm_g, kv_w, k_norm_g, attn_w_q, q_norm_g,
           attn_sinks, attn_w_out):
    batch, seq, d = x.shape
    assert d == D_MODEL and seq % ATT_TQ == 0 and seq % HG_TS == 0 and seq % KV_TM == 0
    assert (batch * seq) % FFN_TM == 0
    h = x.reshape(batch * seq, d)

    ffn_g = ffn_norm_g.reshape(ffn_norm_g.shape[0], 2, 1, D_MODEL)
    w_gu = ffn_w_gate_up.astype(BF16)
    w_d = ffn_w_down.astype(BF16)
    cos_t, sin_lo, sin_hi = _rope_tables(seq)

    h = _ffn(h, ffn_g, w_gu, w_d, 0, 0)
    h = _hgrn(h, mix_norm_g[0].reshape(1, D_MODEL), hgrn_w_in[0].astype(BF16),
              hgrn_lb_logits, hgrn_onorm_g[0].reshape(1, HG_DV),
              hgrn_w_out[0].astype(BF16), batch, seq)
    h = _ffn(h, ffn_g, w_gu, w_d, 0, 1)
    kv = _shared_kv(h, kv_norm_g.reshape(1, D_MODEL), kv_w.astype(BF16),
                    jnp.tile(k_norm_g, HEADS_PER_SLAB).reshape(1, LANES),
                    cos_t, sin_lo, sin_hi, seq)
    h = _ffn(h, ffn_g, w_gu, w_d, 1, 0)
    h = _swa(h, mix_norm_g[1].reshape(1, D_MODEL), attn_w_q[0].astype(BF16),
             jnp.tile(q_norm_g[0], HEADS_PER_SLAB).reshape(1, LANES), attn_sinks[0],
             cos_t, sin_lo, sin_hi, kv, attn_w_out[0].astype(BF16), batch, seq)
    h = _ffn(h, ffn_g, w_gu, w_d, 1, 1)
    return h.reshape(batch, seq, d)
```

```python
import jax
import jax.numpy as jnp
from jax import lax
from jax.experimental import pallas as pl
from jax.experimental.pallas import tpu as pltpu

D_MODEL = 1024
D_FF = 2816
MACARON_WEIGHT = 0.5

HG_HEADS = 8
HG_DK = 128
HG_FD = HG_HEADS * HG_DK
HG_DV = D_MODEL // HG_HEADS
HG_CHUNK = 64

ATT_HEADS = 16
ATT_KV_HEADS = 2
ATT_GROUP = ATT_HEADS // ATT_KV_HEADS
ATT_HEAD_DIM = 64
WINDOW = 128
ATT_SCALE = ATT_HEAD_DIM ** -0.5
ROPE_THETA = 500000.0
ROT_DIM = ATT_HEAD_DIM // 4
EPS = 1e-6

LANES = 128
HEADS_PER_SLAB = LANES // ATT_HEAD_DIM
Q_SLABS = ATT_HEADS // HEADS_PER_SLAB
SLABS_PER_KV = Q_SLABS // ATT_KV_HEADS
MASKED = -1e30
LOG2E = 1.4426950408889634
KV_LANES = 4 * LANES

FFN_TM = 512
FFN_CF = 256
HG_TS = 256
ATT_TQ = 512
VMEM_LIMIT = 56 * 1024 * 1024

BF16 = jnp.bfloat16
F32 = jnp.float32


def _dot(a, b):
    return jnp.dot(a, b, preferred_element_type=F32)


def _dot_nt(a, b):
    return lax.dot_general(a, b, (((1,), (1,)), ((), ())), preferred_element_type=F32)


def _dot_tn(a, b):
    return lax.dot_general(a, b, (((0,), (0,)), ((), ())), preferred_element_type=F32)


def _rms(x, g):
    ms = jnp.mean(x * x, axis=-1, keepdims=True)
    return x * lax.rsqrt(ms + EPS) * g


def _sigmoid(x):
    return 1.0 / (1.0 + jnp.exp(-x))


def _group_mean_sq(x, group_ones, inv_n):
    x2 = x * x
    hi = x2.astype(BF16)
    lo = (x2 - hi.astype(F32)).astype(BF16)
    return (_dot(hi, group_ones) + _dot(lo, group_ones)) * inv_n


def _group_ones(n, group):
    r = lax.broadcasted_iota(jnp.int32, (n, n), 0) // group
    c = lax.broadcasted_iota(jnp.int32, (n, n), 1) // group
    return jnp.where(r == c, 1.0, 0.0).astype(BF16)


def _rope_slab(x, cos_t, sin_lo, sin_hi):
    return (x * cos_t + pltpu.roll(x, LANES - ROT_DIM // 2, axis=1) * sin_lo
            + pltpu.roll(x, ROT_DIM // 2, axis=1) * sin_hi)


def _resident(shape):
    nd = len(shape)
    return pl.BlockSpec(shape, lambda *_: (0,) * nd, pipeline_mode=pl.Buffered(1))


def _swiglu_residual(x, g, wgu_ref, wd_ref):
    hn = _rms(x, g).astype(BF16)
    acc = None
    for c in range(D_FF // FFN_CF):
        lo = c * FFN_CF
        gate = _dot(hn, wgu_ref[:, lo:lo + FFN_CF])
        up = _dot(hn, wgu_ref[:, D_FF + lo:D_FF + lo + FFN_CF])
        act = (gate * _sigmoid(gate) * up).astype(BF16)
        part = _dot(act, wd_ref[lo:lo + FFN_CF, :])
        acc = part if acc is None else acc + part
    return x + MACARON_WEIGHT * acc


def _ffn_kernel(x_ref, g_ref, wgu_ref, wd_ref, o_ref):
    o_ref[...] = _swiglu_residual(x_ref[...], g_ref[...], wgu_ref, wd_ref)


def _ffn_kv_kernel(x_ref, g_ref, wgu_ref, wd_ref, kvg_ref, kvw_ref, kg_ref,
                   cos_ref, slo_ref, shi_ref, o_ref, kv_ref):
    x = x_ref[...]
    hn = _rms(x, kvg_ref[...]).astype(BF16)
    kv = _dot(hn, kvw_ref[...])
    k = kv[:, :LANES]
    v = kv[:, LANES:]
    ms = _group_mean_sq(k, _group_ones(LANES, ATT_HEAD_DIM), 1.0 / ATT_HEAD_DIM)
    k = k * lax.rsqrt(ms + EPS) * kg_ref[...]
    k = _rope_slab(k, cos_ref[...], slo_ref[...], shi_ref[...])
    kv_ref[:, 0 * LANES:1 * LANES] = k.astype(BF16)
    kv_ref[:, 1 * LANES:2 * LANES] = pltpu.roll(k, ATT_HEAD_DIM, axis=1).astype(BF16)
    kv_ref[:, 2 * LANES:3 * LANES] = v.astype(BF16)
    kv_ref[:, 3 * LANES:4 * LANES] = pltpu.roll(v, ATT_HEAD_DIM, axis=1).astype(BF16)
    o_ref[...] = _swiglu_residual(x, g_ref[...], wgu_ref, wd_ref)


def _ffn(h, g, w_gu, w_d, layer, idx, kv_args=None):
    t = h.shape[0]
    in_specs = [
        pl.BlockSpec((FFN_TM, D_MODEL), lambda i: (i, 0)),
        pl.BlockSpec((None, None, 1, D_MODEL), lambda i: (layer, idx, 0, 0)),
        pl.BlockSpec((None, None, D_MODEL, 2 * D_FF), lambda i: (layer, idx, 0, 0),
                     pipeline_mode=pl.Buffered(1)),
        pl.BlockSpec((None, None, D_FF, D_MODEL), lambda i: (layer, idx, 0, 0),
                     pipeline_mode=pl.Buffered(1)),
    ]
    out_shape = jax.ShapeDtypeStruct((t, D_MODEL), F32)
    out_specs = pl.BlockSpec((FFN_TM, D_MODEL), lambda i: (i, 0))
    args = (h, g, w_gu, w_d)
    body = _ffn_kernel
    if kv_args is not None:
        seq = kv_args[-1]
        per_seq = seq // FFN_TM
        table = pl.BlockSpec((FFN_TM, LANES), lambda i: (i % per_seq, 0))
        in_specs += [_resident((1, D_MODEL)), _resident((D_MODEL, 2 * LANES)),
                     _resident((1, LANES)), table, table, table]
        out_shape = (out_shape, jax.ShapeDtypeStruct((t, KV_LANES), BF16))
        out_specs = (out_specs, pl.BlockSpec((FFN_TM, KV_LANES), lambda i: (i, 0)))
        args += tuple(kv_args[:-1])
        body = _ffn_kv_kernel
    return pl.pallas_call(
        body,
        out_shape=out_shape,
        grid=(t // FFN_TM,),
        in_specs=in_specs,
        out_specs=out_specs,
        compiler_params=pltpu.CompilerParams(
            dimension_semantics=("parallel",), vmem_limit_bytes=VMEM_LIMIT),
        name=f"ffn_{layer}_{idx}",
    )(*args)


def _hgrn_kernel(h_ref, g_ref, win_ref, lbl_ref, og_ref, wout_ref, o_ref,
                 qe_ref, ke_ref, kd_ref, v_ref, gate_ref, y_ref, state_ref):
    n_chunks = HG_TS // HG_CHUNK

    @pl.when(pl.program_id(1) == 0)
    def _():
        state_ref[...] = jnp.zeros_like(state_ref)

    x = h_ref[...]
    hn = _rms(x, g_ref[...]).astype(BF16)

    logits = lbl_ref[...]
    top = jnp.max(logits, axis=0, keepdims=True)
    e = jnp.exp(logits - top)
    lower = e[0:1, :] / jnp.sum(e, axis=0, keepdims=True)

    row = lax.broadcasted_iota(jnp.int32, (HG_TS, HG_TS), 0)
    col = lax.broadcasted_iota(jnp.int32, (HG_TS, HG_TS), 1)
    causal = (col <= row) & (col // HG_CHUNK == row // HG_CHUNK)
    tril = jnp.where(causal, 1.0, 0.0).astype(BF16)

    forget = lower + (1.0 - lower) * _sigmoid(_dot(hn, win_ref[:, HG_FD:2 * HG_FD]))
    k = 1.0 - forget
    log_f = jnp.log(forget)
    hi = log_f.astype(BF16)
    lo = (log_f - hi.astype(F32)).astype(BF16)
    b = _dot(tril, hi) + _dot(tril, lo)
    q = _dot(hn, win_ref[:, 0:HG_FD])
    qe_ref[...] = (q * _sigmoid(q) * jnp.exp(b)).astype(BF16)
    ke_ref[...] = (k * jnp.exp(-b)).astype(BF16)
    decay = []
    for c in range(n_chunks):
        rs = slice(c * HG_CHUNK, (c + 1) * HG_CHUNK)
        b_end = b[(c + 1) * HG_CHUNK - 1:(c + 1) * HG_CHUNK, :]
        kd_ref[rs, :] = (k[rs, :] * jnp.exp(b_end - b[rs, :])).astype(BF16)
        decay.append(jnp.exp(b_end))
    v_ref[...] = _dot(hn, win_ref[:, 2 * HG_FD:2 * HG_FD + D_MODEL]).astype(BF16)
    gate = _dot(hn, win_ref[:, 2 * HG_FD + D_MODEL:])
    gate_ref[...] = gate * _sigmoid(gate)

    og = og_ref[...]
    heads = range(HG_HEADS)
    ks = [slice(hd * HG_DK, (hd + 1) * HG_DK) for hd in heads]
    vs = [slice(hd * HG_DV, (hd + 1) * HG_DV) for hd in heads]
    scores = [jnp.where(causal, _dot_nt(qe_ref[:, ks[hd]], ke_ref[:, ks[hd]]), 0.0).astype(BF16)
              for hd in heads]
    o_intra = [_dot(scores[hd], v_ref[:, vs[hd]]) for hd in heads]
    st = [state_ref[hd] for hd in heads]
    o_parts = [[] for _ in heads]
    for c in range(n_chunks):
        rs = slice(c * HG_CHUNK, (c + 1) * HG_CHUNK)
        for hd in heads:
            o_parts[hd].append(o_intra[hd][rs, :]
                               + _dot_nt(qe_ref[rs, ks[hd]], st[hd].astype(BF16)))
        for hd in heads:
            st[hd] = (decay[c][:, ks[hd]] * st[hd]
                      + _dot_tn(v_ref[rs, vs[hd]], kd_ref[rs, ks[hd]]))
    for hd in heads:
        state_ref[hd] = st[hd]
        o = jnp.concatenate(o_parts[hd], axis=0)
        y_ref[:, vs[hd]] = (_rms(o, og) * gate_ref[:, vs[hd]]).astype(BF16)

    o_ref[...] = x + _dot(y_ref[...], wout_ref[...])


def _hgrn(h, g, w_in, lb_logits, onorm_g, w_out, batch, seq):
    steps = seq // HG_TS
    n_lb = lb_logits.shape[0]
    return pl.pallas_call(
        _hgrn_kernel,
        out_shape=jax.ShapeDtypeStruct(h.shape, F32),
        grid=(batch, steps),
        in_specs=[
            pl.BlockSpec((HG_TS, D_MODEL), lambda b, s: (b * steps + s, 0)),
            _resident((1, D_MODEL)),
            _resident((D_MODEL, 2 * HG_FD + 2 * D_MODEL)),
            _resident((n_lb, HG_FD)),
            _resident((1, HG_DV)),
            _resident((D_MODEL, D_MODEL)),
        ],
        out_specs=pl.BlockSpec((HG_TS, D_MODEL), lambda b, s: (b * steps + s, 0)),
        scratch_shapes=[
            pltpu.VMEM((HG_TS, HG_FD), BF16),
            pltpu.VMEM((HG_TS, HG_FD), BF16),
            pltpu.VMEM((HG_TS, HG_FD), BF16),
            pltpu.VMEM((HG_TS, D_MODEL), BF16),
            pltpu.VMEM((HG_TS, D_MODEL), F32),
            pltpu.VMEM((HG_TS, D_MODEL), BF16),
            pltpu.VMEM((HG_HEADS, HG_DV, HG_DK), F32),
        ],
        compiler_params=pltpu.CompilerParams(
            dimension_semantics=("parallel", "arbitrary"), vmem_limit_bytes=VMEM_LIMIT),
        name="hgrn",
    )(h, g, w_in, lb_logits, onorm_g, w_out)


def _swa_kernel(sink_ref, h_ref, g_ref, wq_ref, qg_ref, cos_ref, sin_ref,
                halo_ref, kv_ref, wo_ref, o_ref, q_buf, kvar, vvar, o_buf):
    x = h_ref[...]
    hn = _rms(x, g_ref[...]).astype(BF16)
    q = _dot(hn, wq_ref[...])
    wide = 2 * LANES
    ones = _group_ones(wide, ATT_HEAD_DIM)
    src = lax.broadcasted_iota(jnp.int32, (wide, wide), 0)
    dst = lax.broadcasted_iota(jnp.int32, (wide, wide), 1)
    dim = dst % ATT_HEAD_DIM
    half_rot = ROT_DIM // 2
    partner = jnp.where((dim < half_rot) & (src == dst + half_rot), -1.0,
                        jnp.where((dim >= half_rot) & (dim < ROT_DIM) & (src == dst - half_rot),
                                  1.0, 0.0)).astype(BF16)
    qg = jnp.concatenate([qg_ref[...]] * 2, axis=1)
    cos_t = jnp.concatenate([cos_ref[...]] * 2, axis=1)
    sin_t = jnp.concatenate([sin_ref[...]] * 2, axis=1)
    for s2 in range(Q_SLABS // 2):
        qq = q[:, s2 * wide:(s2 + 1) * wide]
        ms = _dot((qq * qq).astype(BF16), ones) * (1.0 / ATT_HEAD_DIM)
        qn = qq * lax.rsqrt(ms + EPS) * qg
        qs = qn * cos_t + _dot(qn.astype(BF16), partner) * sin_t
        q_buf[:, s2 * wide:(s2 + 1) * wide] = (qs * (ATT_SCALE * LOG2E)).astype(BF16)

    kv_all = jnp.concatenate([halo_ref[...], kv_ref[...]], axis=0)
    k_plain, k_swap = kv_all[:, 0:LANES], kv_all[:, LANES:2 * LANES]
    v_plain, v_swap = kv_all[:, 2 * LANES:3 * LANES], kv_all[:, 3 * LANES:]
    low = lax.broadcasted_iota(jnp.int32, k_plain.shape, 1) < ATT_HEAD_DIM
    zero = jnp.zeros_like(k_plain)
    ones_lo = jnp.where(low, 1.0, 0.0).astype(BF16)
    ones_hi = jnp.where(low, 0.0, 1.0).astype(BF16)
    k_src = ((k_plain, True), (k_swap, False), (k_swap, True), (k_plain, False))
    v_src = ((v_plain, True), (v_swap, False), (v_swap, True), (v_plain, False))
    for i, ((kt, in_low), (vt, _)) in enumerate(zip(k_src, v_src)):
        kvar[:, i * LANES:(i + 1) * LANES] = jnp.where(low, kt, zero) if in_low else jnp.where(low, zero, kt)
        vv = jnp.where(low, vt, zero) if in_low else jnp.where(low, zero, vt)
        vvar[:, 2 * i * LANES:2 * (i + 1) * LANES] = jnp.concatenate(
            [vv, ones_lo if in_low else ones_hi], axis=1)

    cur_ok = (lax.broadcasted_iota(jnp.int32, (WINDOW, WINDOW), 1)
              <= lax.broadcasted_iota(jnp.int32, (WINDOW, WINDOW), 0))
    low_q = lax.broadcasted_iota(jnp.int32, (WINDOW, LANES), 1) < ATT_HEAD_DIM
    at_seq_start = pl.program_id(1) == 0

    def block(qb, carry):
        r0 = pl.multiple_of(qb * WINDOW, WINDOW)
        rows = pl.ds(r0, WINDOW)
        band = pl.ds(r0, 2 * WINDOW)
        prev_bias = jnp.where(jnp.logical_and(at_seq_start, qb == 0), MASKED, 0.0)
        q_st = [jnp.concatenate(
            [q_buf[rows, (kvh * SLABS_PER_KV + p) * LANES:(kvh * SLABS_PER_KV + p + 1) * LANES]
             for p in range(SLABS_PER_KV)], axis=0) for kvh in range(ATT_KV_HEADS)]
        acc = [None] * ATT_KV_HEADS
        sink_e = [[] for _ in range(ATT_KV_HEADS)]
        for par in range(HEADS_PER_SLAB):
            s_all = [_dot_nt(q_st[kvh], kvar[band, (kvh * HEADS_PER_SLAB + par) * LANES:
                                             (kvh * HEADS_PER_SLAB + par + 1) * LANES])
                     for kvh in range(ATT_KV_HEADS)]
            for kvh in range(ATT_KV_HEADS):
                slab = kvh * HEADS_PER_SLAB + par
                es = []
                se = []
                for p in range(SLABS_PER_KV):
                    head = kvh * ATT_GROUP + p * HEADS_PER_SLAB + par
                    sink = sink_ref[head] * LOG2E
                    pr = slice(p * WINDOW, (p + 1) * WINDOW)
                    z = jnp.where(cur_ok, s_all[kvh][pr, WINDOW:], s_all[kvh][pr, 0:WINDOW] + prev_bias)
                    m = jnp.maximum(jnp.max(z, axis=1, keepdims=True), sink)
                    ex = jnp.exp2(z - m)
                    es.append(jnp.concatenate(
                        [jnp.where(cur_ok, 0.0, ex), jnp.where(cur_ok, ex, 0.0)],
                        axis=1).astype(BF16))
                    se.append(jnp.exp2(sink - m))
                e_all = jnp.concatenate(es, axis=0)
                part = _dot(e_all, vvar[band, 2 * slab * LANES:2 * (slab + 1) * LANES])
                acc[kvh] = part if acc[kvh] is None else acc[kvh] + part
                sink_e[kvh].append(se)
        for kvh in range(ATT_KV_HEADS):
            for p in range(SLABS_PER_KV):
                pr = slice(p * WINDOW, (p + 1) * WINDOW)
                den = acc[kvh][pr, LANES:] + jnp.where(low_q, sink_e[kvh][0][p], sink_e[kvh][1][p])
                sl = (kvh * SLABS_PER_KV + p) * LANES
                o_buf[rows, sl:sl + LANES] = (acc[kvh][pr, :LANES] / den).astype(BF16)
        return carry

    lax.fori_loop(0, ATT_TQ // WINDOW, block, 0, unroll=True)
    o_ref[...] = x + _dot(o_buf[...], wo_ref[...])


def _swa(h, g, w_q, qg, sinks, cos_t, sin_t, kv, w_o, batch, seq):
    steps = seq // ATT_TQ
    blocks_per_step = ATT_TQ // WINDOW
    blocks_per_seq = seq // WINDOW

    def halo_map(b, s):
        return (jnp.maximum(b * blocks_per_seq + s * blocks_per_step - 1, 0), 0)

    return pl.pallas_call(
        _swa_kernel,
        out_shape=jax.ShapeDtypeStruct(h.shape, F32),
        grid=(batch, steps),
        in_specs=[
            pl.BlockSpec(memory_space=pltpu.SMEM),
            pl.BlockSpec((ATT_TQ, D_MODEL), lambda b, s: (b * steps + s, 0)),
            _resident((1, D_MODEL)),
            _resident((D_MODEL, D_MODEL)),
            _resident((1, LANES)),
            pl.BlockSpec((ATT_TQ, LANES), lambda b, s: (s, 0)),
            pl.BlockSpec((ATT_TQ, LANES), lambda b, s: (s, 0)),
            pl.BlockSpec((WINDOW, KV_LANES), halo_map),
            pl.BlockSpec((ATT_TQ, KV_LANES), lambda b, s: (b * steps + s, 0)),
            _resident((D_MODEL, D_MODEL)),
        ],
        out_specs=pl.BlockSpec((ATT_TQ, D_MODEL), lambda b, s: (b * steps + s, 0)),
        scratch_shapes=[
            pltpu.VMEM((ATT_TQ, D_MODEL), BF16),
            pltpu.VMEM((ATT_TQ + WINDOW, 4 * LANES), BF16),
            pltpu.VMEM((ATT_TQ + WINDOW, 8 * LANES), BF16),
            pltpu.VMEM((ATT_TQ, D_MODEL), BF16),
        ],
        compiler_params=pltpu.CompilerParams(
            dimension_semantics=("parallel", "parallel"), vmem_limit_bytes=VMEM_LIMIT),
        name="swa",
    )(sinks, h, g, w_q, qg, cos_t, sin_t, kv, kv, w_o)


def _rope_tables(seq):
    half = ROT_DIM // 2
    inv_freq = jnp.power(ROPE_THETA, -jnp.arange(half, dtype=F32) * (2.0 / ROT_DIM))
    ang = jnp.arange(seq).astype(F32)[:, None] * inv_freq[None, :]
    cos, sin = jnp.cos(ang), jnp.sin(ang)
    zeros = jnp.zeros_like(sin)
    rest = ATT_HEAD_DIM - ROT_DIM
    pad0 = jnp.zeros((seq, rest), F32)
    cos_t = jnp.concatenate([cos, cos, jnp.ones((seq, rest), F32)], axis=1)
    sin_lo = jnp.concatenate([-sin, zeros, pad0], axis=1)
    sin_hi = jnp.concatenate([zeros, sin, pad0], axis=1)
    sin_t = jnp.concatenate([sin, sin, pad0], axis=1)
    return tuple(jnp.tile(t, (1, HEADS_PER_SLAB)) for t in (cos_t, sin_lo, sin_hi, sin_t))


def kernel(x, ffn_norm_g, ffn_w_gate_up, ffn_w_down, mix_norm_g, hgrn_w_in, hgrn_lb_logits,
           hgrn_onorm_g, hgrn_w_out, kv_norm_g, kv_w, k_norm_g, attn_w_q, q_norm_g,
           attn_sinks, attn_w_out):
    batch, seq, d = x.shape
    assert d == D_MODEL and seq % ATT_TQ == 0 and seq % HG_TS == 0
    assert (batch * seq) % FFN_TM == 0 and seq % FFN_TM == 0
    h = x.reshape(batch * seq, d)

    ffn_g = ffn_norm_g.reshape(ffn_norm_g.shape[0], 2, 1, D_MODEL)
    w_gu = ffn_w_gate_up.astype(BF16)
    w_d = ffn_w_down.astype(BF16)
    cos_t, sin_lo, sin_hi, sin_t = _rope_tables(seq)

    h = _ffn(h, ffn_g, w_gu, w_d, 0, 0)
    h = _hgrn(h, mix_norm_g[0].reshape(1, D_MODEL), hgrn_w_in[0].astype(BF16),
              hgrn_lb_logits, hgrn_onorm_g[0].reshape(1, HG_DV),
              hgrn_w_out[0].astype(BF16), batch, seq)
    h = _ffn(h, ffn_g, w_gu, w_d, 0, 1)
    h, kv = _ffn(h, ffn_g, w_gu, w_d, 1, 0,
                 kv_args=(kv_norm_g.reshape(1, D_MODEL), kv_w.astype(BF16),
                          jnp.tile(k_norm_g, HEADS_PER_SLAB).reshape(1, LANES),
                          cos_t, sin_lo, sin_hi, seq))
    h = _swa(h, mix_norm_g[1].reshape(1, D_MODEL), attn_w_q[0].astype(BF16),
             jnp.tile(q_norm_g[0], HEADS_PER_SLAB).reshape(1, LANES), attn_sinks[0],
             cos_t, sin_t, kv, attn_w_out[0].astype(BF16), batch, seq)
    h = _ffn(h, ffn_g, w_gu, w_d, 1, 1)
    return h.reshape(batch, seq, d)
```

```python
import jax
import jax.numpy as jnp
from jax import lax
from jax.experimental import pallas as pl
from jax.experimental.pallas import tpu as pltpu

D_MODEL = 1024
D_FF = 2816
MACARON_WEIGHT = 0.5

HG_HEADS = 8
HG_DK = 128
HG_FD = HG_HEADS * HG_DK
HG_DV = D_MODEL // HG_HEADS
HG_CHUNK = 64

ATT_HEADS = 16
ATT_KV_HEADS = 2
ATT_GROUP = ATT_HEADS // ATT_KV_HEADS
ATT_HEAD_DIM = 64
WINDOW = 128
ATT_SCALE = ATT_HEAD_DIM ** -0.5
ROPE_THETA = 500000.0
ROT_DIM = ATT_HEAD_DIM // 4
EPS = 1e-6

LANES = 128
HEADS_PER_SLAB = LANES // ATT_HEAD_DIM
Q_SLABS = ATT_HEADS // HEADS_PER_SLAB
SLABS_PER_KV = Q_SLABS // ATT_KV_HEADS
MASKED = -1e30
LOG2E = 1.4426950408889634
KV_LANES = 4 * LANES

FFN_TM = 1024
FFN_SPLIT = 2
FFN_CF = 256
HG_TS = 256
HG_SEQS = 2
ATT_TQ = 512
VMEM_LIMIT = 56 * 1024 * 1024

BF16 = jnp.bfloat16
F32 = jnp.float32


def _dot(a, b):
    return jnp.dot(a, b, preferred_element_type=F32)


def _dot_nt(a, b):
    return lax.dot_general(a, b, (((1,), (1,)), ((), ())), preferred_element_type=F32)


def _dot_tn(a, b):
    return lax.dot_general(a, b, (((0,), (0,)), ((), ())), preferred_element_type=F32)


def _rms(x, g):
    ms = jnp.mean(x * x, axis=-1, keepdims=True)
    return x * lax.rsqrt(ms + EPS) * g


def _sigmoid(x):
    return 1.0 / (1.0 + jnp.exp(-x))


def _group_mean_sq(x, group_ones, inv_n):
    x2 = x * x
    hi = x2.astype(BF16)
    lo = (x2 - hi.astype(F32)).astype(BF16)
    return (_dot(hi, group_ones) + _dot(lo, group_ones)) * inv_n


def _group_ones(n, group):
    r = lax.broadcasted_iota(jnp.int32, (n, n), 0) // group
    c = lax.broadcasted_iota(jnp.int32, (n, n), 1) // group
    return jnp.where(r == c, 1.0, 0.0).astype(BF16)


def _rope_slab(x, cos_t, sin_lo, sin_hi):
    return (x * cos_t + pltpu.roll(x, LANES - ROT_DIM // 2, axis=1) * sin_lo
            + pltpu.roll(x, ROT_DIM // 2, axis=1) * sin_hi)


def _resident(shape):
    nd = len(shape)
    return pl.BlockSpec(shape, lambda *_: (0,) * nd, pipeline_mode=pl.Buffered(1))


def _swiglu_residual(x_ref, g, wgu_ref, wd_ref, o_ref):
    rows = FFN_TM // FFN_SPLIT
    groups = [slice(i * rows, (i + 1) * rows) for i in range(FFN_SPLIT)]
    hns = [_rms(x_ref[r, :], g).astype(BF16) for r in groups]
    accs = [None] * FFN_SPLIT
    for c in range(D_FF // FFN_CF):
        lo = c * FFN_CF
        for i, hn in enumerate(hns):
            gate = _dot(hn, wgu_ref[:, lo:lo + FFN_CF])
            up = _dot(hn, wgu_ref[:, D_FF + lo:D_FF + lo + FFN_CF])
            act = (gate * _sigmoid(gate) * up).astype(BF16)
            part = _dot(act, wd_ref[lo:lo + FFN_CF, :])
            accs[i] = part if accs[i] is None else accs[i] + part
    for r, acc in zip(groups, accs):
        o_ref[r, :] = x_ref[r, :] + MACARON_WEIGHT * acc


def _ffn_kernel(x_ref, g_ref, wgu_ref, wd_ref, o_ref):
    _swiglu_residual(x_ref, g_ref[...], wgu_ref, wd_ref, o_ref)


def _ffn_kv_kernel(x_ref, g_ref, wgu_ref, wd_ref, kvg_ref, kvw_ref, kg_ref,
                   cos_ref, slo_ref, shi_ref, o_ref, kv_ref):
    rows = FFN_TM // FFN_SPLIT
    ones = _group_ones(LANES, ATT_HEAD_DIM)
    for i in range(FFN_SPLIT):
        r = slice(i * rows, (i + 1) * rows)
        hn = _rms(x_ref[r, :], kvg_ref[...]).astype(BF16)
        kv = _dot(hn, kvw_ref[...])
        k = kv[:, :LANES]
        v = kv[:, LANES:]
        ms = _group_mean_sq(k, ones, 1.0 / ATT_HEAD_DIM)
        k = k * lax.rsqrt(ms + EPS) * kg_ref[...]
        k = _rope_slab(k, cos_ref[r, :], slo_ref[r, :], shi_ref[r, :])
        kv_ref[r, 0 * LANES:1 * LANES] = k.astype(BF16)
        kv_ref[r, 1 * LANES:2 * LANES] = pltpu.roll(k, ATT_HEAD_DIM, axis=1).astype(BF16)
        kv_ref[r, 2 * LANES:3 * LANES] = v.astype(BF16)
        kv_ref[r, 3 * LANES:4 * LANES] = pltpu.roll(v, ATT_HEAD_DIM, axis=1).astype(BF16)
    _swiglu_residual(x_ref, g_ref[...], wgu_ref, wd_ref, o_ref)


def _ffn(h, g, w_gu, w_d, layer, idx, kv_args=None):
    t = h.shape[0]
    in_specs = [
        pl.BlockSpec((FFN_TM, D_MODEL), lambda i: (i, 0)),
        pl.BlockSpec((None, None, 1, D_MODEL), lambda i: (layer, idx, 0, 0)),
        pl.BlockSpec((None, None, D_MODEL, 2 * D_FF), lambda i: (layer, idx, 0, 0),
                     pipeline_mode=pl.Buffered(1)),
        pl.BlockSpec((None, None, D_FF, D_MODEL), lambda i: (layer, idx, 0, 0),
                     pipeline_mode=pl.Buffered(1)),
    ]
    out_shape = jax.ShapeDtypeStruct((t, D_MODEL), F32)
    out_specs = pl.BlockSpec((FFN_TM, D_MODEL), lambda i: (i, 0))
    args = (h, g, w_gu, w_d)
    body = _ffn_kernel
    if kv_args is not None:
        seq = kv_args[-1]
        per_seq = seq // FFN_TM
        table = pl.BlockSpec((FFN_TM, LANES), lambda i: (i % per_seq, 0))
        in_specs += [_resident((1, D_MODEL)), _resident((D_MODEL, 2 * LANES)),
                     _resident((1, LANES)), table, table, table]
        out_shape = (out_shape, jax.ShapeDtypeStruct((t, KV_LANES), BF16))
        out_specs = (out_specs, pl.BlockSpec((FFN_TM, KV_LANES), lambda i: (i, 0)))
        args += tuple(kv_args[:-1])
        body = _ffn_kv_kernel
    return pl.pallas_call(
        body,
        out_shape=out_shape,
        grid=(t // FFN_TM,),
        in_specs=in_specs,
        out_specs=out_specs,
        compiler_params=pltpu.CompilerParams(
            dimension_semantics=("parallel",), vmem_limit_bytes=VMEM_LIMIT),
        name=f"ffn_{layer}_{idx}",
    )(*args)


def _hgrn_kernel(h_ref, g_ref, win_ref, lbl_ref, og_ref, wout_ref, o_ref,
                 qe_ref, ke_ref, kd_ref, v_ref, gate_ref, y_ref, state_ref):
    n_chunks = HG_TS // HG_CHUNK
    seqs = range(HG_SEQS)

    @pl.when(pl.program_id(1) == 0)
    def _():
        state_ref[...] = jnp.zeros_like(state_ref)

    logits = lbl_ref[...]
    top = jnp.max(logits, axis=0, keepdims=True)
    e = jnp.exp(logits - top)
    lower = e[0:1, :] / jnp.sum(e, axis=0, keepdims=True)

    row = lax.broadcasted_iota(jnp.int32, (HG_TS, HG_TS), 0)
    col = lax.broadcasted_iota(jnp.int32, (HG_TS, HG_TS), 1)
    causal = (col <= row) & (col // HG_CHUNK == row // HG_CHUNK)
    tril = jnp.where(causal, 1.0, 0.0).astype(BF16)

    g = g_ref[...]
    hn = [_rms(h_ref[t], g).astype(BF16) for t in seqs]

    forget = [lower + (1.0 - lower) * _sigmoid(_dot(hn[t], win_ref[:, HG_FD:2 * HG_FD]))
              for t in seqs]
    k = [1.0 - forget[t] for t in seqs]
    log_f = [jnp.log(forget[t]) for t in seqs]
    hi = [log_f[t].astype(BF16) for t in seqs]
    lo = [(log_f[t] - hi[t].astype(F32)).astype(BF16) for t in seqs]
    b = [_dot(tril, hi[t]) + _dot(tril, lo[t]) for t in seqs]
    q = [_dot(hn[t], win_ref[:, 0:HG_FD]) for t in seqs]
    for t in seqs:
        qe_ref[t] = (q[t] * _sigmoid(q[t]) * jnp.exp(b[t])).astype(BF16)
        ke_ref[t] = (k[t] * jnp.exp(-b[t])).astype(BF16)
    for t in seqs:
        v_ref[t] = _dot(hn[t], win_ref[:, 2 * HG_FD:2 * HG_FD + D_MODEL]).astype(BF16)
    decay = [[] for _ in seqs]
    for t in seqs:
        for c in range(n_chunks):
            rs = slice(c * HG_CHUNK, (c + 1) * HG_CHUNK)
            b_end = b[t][(c + 1) * HG_CHUNK - 1:(c + 1) * HG_CHUNK, :]
            kd_ref[t, rs, :] = (k[t][rs, :] * jnp.exp(b_end - b[t][rs, :])).astype(BF16)
            decay[t].append(jnp.exp(b_end))
    for t in seqs:
        gate = _dot(hn[t], win_ref[:, 2 * HG_FD + D_MODEL:])
        gate_ref[t] = gate * _sigmoid(gate)

    og = og_ref[...]
    units = [(t, hd) for hd in range(HG_HEADS) for t in seqs]
    ks = [slice(hd * HG_DK, (hd + 1) * HG_DK) for hd in range(HG_HEADS)]
    vs = [slice(hd * HG_DV, (hd + 1) * HG_DV) for hd in range(HG_HEADS)]
    scores = {u: jnp.where(causal, _dot_nt(qe_ref[u[0], :, ks[u[1]]], ke_ref[u[0], :, ks[u[1]]]),
                           0.0).astype(BF16) for u in units}
    o_intra = {u: _dot(scores[u], v_ref[u[0], :, vs[u[1]]]) for u in units}
    st = {u: state_ref[u[0], u[1]] for u in units}
    o_parts = {u: [] for u in units}
    for c in range(n_chunks):
        rs = slice(c * HG_CHUNK, (c + 1) * HG_CHUNK)
        for u in units:
            t, hd = u
            o_parts[u].append(o_intra[u][rs, :]
                              + _dot_nt(qe_ref[t, rs, ks[hd]], st[u].astype(BF16)))
        for u in units:
            t, hd = u
            st[u] = (decay[t][c][:, ks[hd]] * st[u]
                     + _dot_tn(v_ref[t, rs, vs[hd]], kd_ref[t, rs, ks[hd]]))
    for u in units:
        t, hd = u
        state_ref[t, hd] = st[u]
        o = jnp.concatenate(o_parts[u], axis=0)
        y_ref[t, :, vs[hd]] = (_rms(o, og) * gate_ref[t, :, vs[hd]]).astype(BF16)

    for t in seqs:
        o_ref[t] = h_ref[t] + _dot(y_ref[t], wout_ref[...])


def _hgrn(h, g, w_in, lb_logits, onorm_g, w_out):
    batch, seq, _ = h.shape
    steps = seq // HG_TS
    n_lb = lb_logits.shape[0]
    tile = pl.BlockSpec((HG_SEQS, HG_TS, D_MODEL), lambda b, s: (b, s, 0))
    return pl.pallas_call(
        _hgrn_kernel,
        out_shape=jax.ShapeDtypeStruct(h.shape, F32),
        grid=(batch // HG_SEQS, steps),
        in_specs=[
            tile,
            _resident((1, D_MODEL)),
            _resident((D_MODEL, 2 * HG_FD + 2 * D_MODEL)),
            _resident((n_lb, HG_FD)),
            _resident((1, HG_DV)),
            _resident((D_MODEL, D_MODEL)),
        ],
        out_specs=tile,
        scratch_shapes=[
            pltpu.VMEM((HG_SEQS, HG_TS, HG_FD), BF16),
            pltpu.VMEM((HG_SEQS, HG_TS, HG_FD), BF16),
            pltpu.VMEM((HG_SEQS, HG_TS, HG_FD), BF16),
            pltpu.VMEM((HG_SEQS, HG_TS, D_MODEL), BF16),
            pltpu.VMEM((HG_SEQS, HG_TS, D_MODEL), F32),
            pltpu.VMEM((HG_SEQS, HG_TS, D_MODEL), BF16),
            pltpu.VMEM((HG_SEQS, HG_HEADS, HG_DV, HG_DK), F32),
        ],
        compiler_params=pltpu.CompilerParams(
            dimension_semantics=("parallel", "arbitrary"), vmem_limit_bytes=VMEM_LIMIT),
        name="hgrn",
    )(h, g, w_in, lb_logits, onorm_g, w_out)


def _swa_kernel(sink_ref, h_ref, g_ref, wq_ref, qg_ref, cos_ref, sin_ref,
                halo_ref, kv_ref, wo_ref, o_ref, q_buf, kvar, vvar, o_buf):
    x = h_ref[...]
    hn = _rms(x, g_ref[...]).astype(BF16)
    q = _dot(hn, wq_ref[...])
    wide = 2 * LANES
    ones = _group_ones(wide, ATT_HEAD_DIM)
    src = lax.broadcasted_iota(jnp.int32, (wide, wide), 0)
    dst = lax.broadcasted_iota(jnp.int32, (wide, wide), 1)
    dim = dst % ATT_HEAD_DIM
    half_rot = ROT_DIM // 2
    partner = jnp.where((dim < half_rot) & (src == dst + half_rot), -1.0,
                        jnp.where((dim >= half_rot) & (dim < ROT_DIM) & (src == dst - half_rot),
                                  1.0, 0.0)).astype(BF16)
    qg = jnp.concatenate([qg_ref[...]] * 2, axis=1)
    cos_t = jnp.concatenate([cos_ref[...]] * 2, axis=1)
    sin_t = jnp.concatenate([sin_ref[...]] * 2, axis=1)
    for s2 in range(Q_SLABS // 2):
        qq = q[:, s2 * wide:(s2 + 1) * wide]
        ms = _dot((qq * qq).astype(BF16), ones) * (1.0 / ATT_HEAD_DIM)
        qn = qq * lax.rsqrt(ms + EPS) * qg
        qs = qn * cos_t + _dot(qn.astype(BF16), partner) * sin_t
        q_buf[:, s2 * wide:(s2 + 1) * wide] = (qs * (ATT_SCALE * LOG2E)).astype(BF16)

    kv_all = jnp.concatenate([halo_ref[...], kv_ref[...]], axis=0)
    k_plain, k_swap = kv_all[:, 0:LANES], kv_all[:, LANES:2 * LANES]
    v_plain, v_swap = kv_all[:, 2 * LANES:3 * LANES], kv_all[:, 3 * LANES:]
    low = lax.broadcasted_iota(jnp.int32, k_plain.shape, 1) < ATT_HEAD_DIM
    zero = jnp.zeros_like(k_plain)
    ones_lo = jnp.where(low, 1.0, 0.0).astype(BF16)
    ones_hi = jnp.where(low, 0.0, 1.0).astype(BF16)
    k_src = ((k_plain, True), (k_swap, False), (k_swap, True), (k_plain, False))
    v_src = ((v_plain, True), (v_swap, False), (v_swap, True), (v_plain, False))
    for i, ((kt, in_low), (vt, _)) in enumerate(zip(k_src, v_src)):
        kvar[:, i * LANES:(i + 1) * LANES] = jnp.where(low, kt, zero) if in_low else jnp.where(low, zero, kt)
        vv = jnp.where(low, vt, zero) if in_low else jnp.where(low, zero, vt)
        vvar[:, 2 * i * LANES:2 * (i + 1) * LANES] = jnp.concatenate(
            [vv, ones_lo if in_low else ones_hi], axis=1)

    cur_ok = (lax.broadcasted_iota(jnp.int32, (WINDOW, WINDOW), 1)
              <= lax.broadcasted_iota(jnp.int32, (WINDOW, WINDOW), 0))
    low_q = lax.broadcasted_iota(jnp.int32, (WINDOW, LANES), 1) < ATT_HEAD_DIM
    at_seq_start = pl.program_id(1) == 0

    def block(qb, carry):
        r0 = pl.multiple_of(qb * WINDOW, WINDOW)
        rows = pl.ds(r0, WINDOW)
        band = pl.ds(r0, 2 * WINDOW)
        prev_bias = jnp.where(jnp.logical_and(at_seq_start, qb == 0), MASKED, 0.0)
        q_st = [jnp.concatenate(
            [q_buf[rows, (kvh * SLABS_PER_KV + p) * LANES:(kvh * SLABS_PER_KV + p + 1) * LANES]
             for p in range(SLABS_PER_KV)], axis=0) for kvh in range(ATT_KV_HEADS)]
        acc = [None] * ATT_KV_HEADS
        sink_e = [[] for _ in range(ATT_KV_HEADS)]
        for par in range(HEADS_PER_SLAB):
            s_all = [_dot_nt(q_st[kvh], kvar[band, (kvh * HEADS_PER_SLAB + par) * LANES:
                                             (kvh * HEADS_PER_SLAB + par + 1) * LANES])
                     for kvh in range(ATT_KV_HEADS)]
            for kvh in range(ATT_KV_HEADS):
                slab = kvh * HEADS_PER_SLAB + par
                es = []
                se = []
                for p in range(SLABS_PER_KV):
                    head = kvh * ATT_GROUP + p * HEADS_PER_SLAB + par
                    sink = sink_ref[head] * LOG2E
                    pr = slice(p * WINDOW, (p + 1) * WINDOW)
                    z = jnp.where(cur_ok, s_all[kvh][pr, WINDOW:], s_all[kvh][pr, 0:WINDOW] + prev_bias)
                    m = jnp.maximum(jnp.max(z, axis=1, keepdims=True), sink)
                    ex = jnp.exp2(z - m)
                    es.append(jnp.concatenate(
                        [jnp.where(cur_ok, 0.0, ex), jnp.where(cur_ok, ex, 0.0)],
                        axis=1).astype(BF16))
                    se.append(jnp.exp2(sink - m))
                e_all = jnp.concatenate(es, axis=0)
                part = _dot(e_all, vvar[band, 2 * slab * LANES:2 * (slab + 1) * LANES])
                acc[kvh] = part if acc[kvh] is None else acc[kvh] + part
                sink_e[kvh].append(se)
        for kvh in range(ATT_KV_HEADS):
            for p in range(SLABS_PER_KV):
                pr = slice(p * WINDOW, (p + 1) * WINDOW)
                den = acc[kvh][pr, LANES:] + jnp.where(low_q, sink_e[kvh][0][p], sink_e[kvh][1][p])
                sl = (kvh * SLABS_PER_KV + p) * LANES
                o_buf[rows, sl:sl + LANES] = (acc[kvh][pr, :LANES] / den).astype(BF16)
        return carry

    lax.fori_loop(0, ATT_TQ // WINDOW, block, 0, unroll=True)
    o_ref[...] = x + _dot(o_buf[...], wo_ref[...])


def _swa(h, g, w_q, qg, sinks, cos_t, sin_t, kv, w_o, batch, seq):
    steps = seq // ATT_TQ
    blocks_per_step = ATT_TQ // WINDOW
    blocks_per_seq = seq // WINDOW

    def halo_map(b, s):
        return (jnp.maximum(b * blocks_per_seq + s * blocks_per_step - 1, 0), 0)

    return pl.pallas_call(
        _swa_kernel,
        out_shape=jax.ShapeDtypeStruct(h.shape, F32),
        grid=(batch, steps),
        in_specs=[
            pl.BlockSpec(memory_space=pltpu.SMEM),
            pl.BlockSpec((ATT_TQ, D_MODEL), lambda b, s: (b * steps + s, 0)),
            _resident((1, D_MODEL)),
            _resident((D_MODEL, D_MODEL)),
            _resident((1, LANES)),
            pl.BlockSpec((ATT_TQ, LANES), lambda b, s: (s, 0)),
            pl.BlockSpec((ATT_TQ, LANES), lambda b, s: (s, 0)),
            pl.BlockSpec((WINDOW, KV_LANES), halo_map),
            pl.BlockSpec((ATT_TQ, KV_LANES), lambda b, s: (b * steps + s, 0)),
            _resident((D_MODEL, D_MODEL)),
        ],
        out_specs=pl.BlockSpec((ATT_TQ, D_MODEL), lambda b, s: (b * steps + s, 0)),
        scratch_shapes=[
            pltpu.VMEM((ATT_TQ, D_MODEL), BF16),
            pltpu.VMEM((ATT_TQ + WINDOW, 4 * LANES), BF16),
            pltpu.VMEM((ATT_TQ + WINDOW, 8 * LANES), BF16),
            pltpu.VMEM((ATT_TQ, D_MODEL), BF16),
        ],
        compiler_params=pltpu.CompilerParams(
            dimension_semantics=("parallel", "parallel"), vmem_limit_bytes=VMEM_LIMIT),
        name="swa",
    )(sinks, h, g, w_q, qg, cos_t, sin_t, kv, kv, w_o)


def _rope_tables(seq):
    half = ROT_DIM // 2
    inv_freq = jnp.power(ROPE_THETA, -jnp.arange(half, dtype=F32) * (2.0 / ROT_DIM))
    ang = jnp.arange(seq).astype(F32)[:, None] * inv_freq[None, :]
    cos, sin = jnp.cos(ang), jnp.sin(ang)
    zeros = jnp.zeros_like(sin)
    rest = ATT_HEAD_DIM - ROT_DIM
    pad0 = jnp.zeros((seq, rest), F32)
    cos_t = jnp.concatenate([cos, cos, jnp.ones((seq, rest), F32)], axis=1)
    sin_lo = jnp.concatenate([-sin, zeros, pad0], axis=1)
    sin_hi = jnp.concatenate([zeros, sin, pad0], axis=1)
    sin_t = jnp.concatenate([sin, sin, pad0], axis=1)
    return tuple(jnp.tile(t, (1, HEADS_PER_SLAB)) for t in (cos_t, sin_lo, sin_hi, sin_t))


def kernel(x, ffn_norm_g, ffn_w_gate_up, ffn_w_down, mix_norm_g, hgrn_w_in, hgrn_lb_logits,
           hgrn_onorm_g, hgrn_w_out, kv_norm_g, kv_w, k_norm_g, attn_w_q, q_norm_g,
           attn_sinks, attn_w_out):
    batch, seq, d = x.shape
    assert d == D_MODEL and seq % ATT_TQ == 0 and seq % HG_TS == 0 and batch % HG_SEQS == 0
    assert (batch * seq) % FFN_TM == 0 and seq % FFN_TM == 0
    h = x.reshape(batch * seq, d)

    ffn_g = ffn_norm_g.reshape(ffn_norm_g.shape[0], 2, 1, D_MODEL)
    w_gu = ffn_w_gate_up.astype(BF16)
    w_d = ffn_w_down.astype(BF16)
    cos_t, sin_lo, sin_hi, sin_t = _rope_tables(seq)

    h = _ffn(h, ffn_g, w_gu, w_d, 0, 0)
    h = _hgrn(h.reshape(batch, seq, d), mix_norm_g[0].reshape(1, D_MODEL),
              hgrn_w_in[0].astype(BF16), hgrn_lb_logits, hgrn_onorm_g[0].reshape(1, HG_DV),
              hgrn_w_out[0].astype(BF16)).reshape(batch * seq, d)
    h = _ffn(h, ffn_g, w_gu, w_d, 0, 1)
    h, kv = _ffn(h, ffn_g, w_gu, w_d, 1, 0,
                 kv_args=(kv_norm_g.reshape(1, D_MODEL), kv_w.astype(BF16),
                          jnp.tile(k_norm_g, HEADS_PER_SLAB).reshape(1, LANES),
                          cos_t, sin_lo, sin_hi, seq))
    h = _swa(h, mix_norm_g[1].reshape(1, D_MODEL), attn_w_q[0].astype(BF16),
             jnp.tile(q_norm_g[0], HEADS_PER_SLAB).reshape(1, LANES), attn_sinks[0],
             cos_t, sin_t, kv, attn_w_out[0].astype(BF16), batch, seq)
    h = _ffn(h, ffn_g, w_gu, w_d, 1, 1)
    return h.reshape(batch, seq, d)
```

```python
import functools

import jax
import jax.numpy as jnp
from jax import lax
from jax.experimental import pallas as pl
from jax.experimental.pallas import tpu as pltpu

D_MODEL = 1024
D_FF = 2816
MACARON_WEIGHT = 0.5

HG_HEADS = 8
HG_DK = 128
HG_FD = HG_HEADS * HG_DK
HG_DV = D_MODEL // HG_HEADS
HG_CHUNK = 64

ATT_HEADS = 16
ATT_KV_HEADS = 2
ATT_GROUP = ATT_HEADS // ATT_KV_HEADS
ATT_HEAD_DIM = 64
WINDOW = 128
ATT_SCALE = ATT_HEAD_DIM ** -0.5
ROPE_THETA = 500000.0
ROT_DIM = ATT_HEAD_DIM // 4
EPS = 1e-6

LANES = 128
BF16_SUBLANES = 16
HEADS_PER_SLAB = LANES // ATT_HEAD_DIM
Q_SLABS = ATT_HEADS // HEADS_PER_SLAB
SLABS_PER_KV = Q_SLABS // ATT_KV_HEADS
MASKED = -1e30
LOG2E = 1.4426950408889634
KV_LANES = 4 * LANES

FFN_TM = 1024
FFN_SPLIT = 2
FFN_CF = 256
HG_TS = 256
HG_SEQS = 2
ATT_TQ = 512
VMEM_LIMIT = 56 * 1024 * 1024

BF16 = jnp.bfloat16
F32 = jnp.float32


def _dot(a, b):
    return jnp.dot(a, b, preferred_element_type=F32)


def _dot_nt(a, b):
    return lax.dot_general(a, b, (((1,), (1,)), ((), ())), preferred_element_type=F32)


def _dot_tn(a, b):
    return lax.dot_general(a, b, (((0,), (0,)), ((), ())), preferred_element_type=F32)


def _rms(x, g):
    ms = jnp.mean(x * x, axis=-1, keepdims=True)
    return x * lax.rsqrt(ms + EPS) * g


def _sigmoid(x):
    return 1.0 / (1.0 + jnp.exp(-x))


def _group_mean_sq(x, group_ones, inv_n):
    x2 = x * x
    hi = x2.astype(BF16)
    lo = (x2 - hi.astype(F32)).astype(BF16)
    return (_dot(hi, group_ones) + _dot(lo, group_ones)) * inv_n


def _group_ones(n, group):
    r = lax.broadcasted_iota(jnp.int32, (n, n), 0) // group
    c = lax.broadcasted_iota(jnp.int32, (n, n), 1) // group
    return jnp.where(r == c, 1.0, 0.0).astype(BF16)


def _rope_slab(x, cos_t, sin_lo, sin_hi):
    return (x * cos_t + pltpu.roll(x, LANES - ROT_DIM // 2, axis=1) * sin_lo
            + pltpu.roll(x, ROT_DIM // 2, axis=1) * sin_hi)


def _resident(shape):
    nd = len(shape)
    return pl.BlockSpec(shape, lambda *_: (0,) * nd, pipeline_mode=pl.Buffered(1))


def _swiglu_residual(x_ref, g, wgu_ref, wd_ref, o_ref):
    rows = FFN_TM // FFN_SPLIT
    groups = [slice(i * rows, (i + 1) * rows) for i in range(FFN_SPLIT)]
    hns = [_rms(x_ref[r, :], g).astype(BF16) for r in groups]
    accs = [None] * FFN_SPLIT
    for c in range(D_FF // FFN_CF):
        lo = c * FFN_CF
        for i, hn in enumerate(hns):
            gate = _dot(hn, wgu_ref[:, lo:lo + FFN_CF])
            up = _dot(hn, wgu_ref[:, D_FF + lo:D_FF + lo + FFN_CF])
            act = (gate * _sigmoid(gate) * up).astype(BF16)
            part = _dot(act, wd_ref[lo:lo + FFN_CF, :])
            accs[i] = part if accs[i] is None else accs[i] + part
    for r, acc in zip(groups, accs):
        o_ref[r, :] = x_ref[r, :] + MACARON_WEIGHT * acc


def _shared_kv(x_ref, kvg_ref, kvw_ref, kg_ref, cos_ref, slo_ref, shi_ref, kv_ref):
    rows = FFN_TM // FFN_SPLIT
    ones = _group_ones(LANES, ATT_HEAD_DIM)
    for i in range(FFN_SPLIT):
        r = slice(i * rows, (i + 1) * rows)
        hn = _rms(x_ref[r, :], kvg_ref[...]).astype(BF16)
        kv = _dot(hn, kvw_ref[...])
        k = kv[:, :LANES]
        v = kv[:, LANES:]
        ms = _group_mean_sq(k, ones, 1.0 / ATT_HEAD_DIM)
        k = k * lax.rsqrt(ms + EPS) * kg_ref[...]
        k = _rope_slab(k, cos_ref[r, :], slo_ref[r, :], shi_ref[r, :])
        kv_ref[r, 0 * LANES:1 * LANES] = k.astype(BF16)
        kv_ref[r, 1 * LANES:2 * LANES] = pltpu.roll(k, ATT_HEAD_DIM, axis=1).astype(BF16)
        kv_ref[r, 2 * LANES:3 * LANES] = v.astype(BF16)
        kv_ref[r, 3 * LANES:4 * LANES] = pltpu.roll(v, ATT_HEAD_DIM, axis=1).astype(BF16)


def _ffn_kernel(*refs, with_kv, n_cast):
    n_in = 4 + (6 if with_kv else 0) + n_cast
    ins, outs = refs[:n_in], refs[n_in:]
    x_ref, g_ref, wgu_ref, wd_ref = ins[:4]
    for src_ref, dst_ref in zip(ins[n_in - n_cast:], outs[len(outs) - n_cast:]):
        dst_ref[...] = src_ref[...].astype(BF16)
    if with_kv:
        _shared_kv(x_ref, *ins[4:10], outs[1])
    _swiglu_residual(x_ref, g_ref[...], wgu_ref, wd_ref, outs[0])


def _cast_rows(n_rows, steps):
    rows = BF16_SUBLANES
    while n_rows % rows or n_rows // rows > steps:
        rows += BF16_SUBLANES
    return rows


def _ffn(h, g, w_gu, w_d, name, kv_args=None, casts=()):
    t = h.shape[0]
    steps = t // FFN_TM
    in_specs = [
        pl.BlockSpec((FFN_TM, D_MODEL), lambda i: (i, 0)),
        _resident((1, D_MODEL)),
        _resident((D_MODEL, 2 * D_FF)),
        _resident((D_FF, D_MODEL)),
    ]
    out_shape = [jax.ShapeDtypeStruct((t, D_MODEL), F32)]
    out_specs = [pl.BlockSpec((FFN_TM, D_MODEL), lambda i: (i, 0))]
    args = [h, g, w_gu, w_d]
    if kv_args is not None:
        seq = kv_args[-1]
        per_seq = seq // FFN_TM
        table = pl.BlockSpec((FFN_TM, LANES), lambda i: (i % per_seq, 0))
        in_specs += [_resident((1, D_MODEL)), _resident((D_MODEL, 2 * LANES)),
                     _resident((1, LANES)), table, table, table]
        out_shape.append(jax.ShapeDtypeStruct((t, KV_LANES), BF16))
        out_specs.append(pl.BlockSpec((FFN_TM, KV_LANES), lambda i: (i, 0)))
        args += list(kv_args[:-1])
    for w, lead in casts:
        n_rows, cols = w.shape[len(lead):]
        rows = _cast_rows(n_rows, steps)
        last = n_rows // rows - 1
        in_specs.append(pl.BlockSpec(
            (None,) * len(lead) + (rows, cols),
            lambda i, lead=lead, last=last: lead + (jnp.minimum(i, last), 0)))
        out_specs.append(pl.BlockSpec((rows, cols), lambda i, last=last: (jnp.minimum(i, last), 0)))
        out_shape.append(jax.ShapeDtypeStruct((n_rows, cols), BF16))
        args.append(w)
    return pl.pallas_call(
        functools.partial(_ffn_kernel, with_kv=kv_args is not None, n_cast=len(casts)),
        out_shape=tuple(out_shape),
        grid=(steps,),
        in_specs=in_specs,
        out_specs=tuple(out_specs),
        compiler_params=pltpu.CompilerParams(
            dimension_semantics=("arbitrary",), vmem_limit_bytes=VMEM_LIMIT),
        name=name,
    )(*args)


def _hgrn_kernel(h_ref, g_ref, win_ref, lbl_ref, og_ref, wout_ref, o_ref,
                 qe_ref, ke_ref, kd_ref, v_ref, gate_ref, y_ref, state_ref):
    n_chunks = HG_TS // HG_CHUNK
    seqs = range(HG_SEQS)

    @pl.when(pl.program_id(1) == 0)
    def _():
        state_ref[...] = jnp.zeros_like(state_ref)

    logits = lbl_ref[...]
    top = jnp.max(logits, axis=0, keepdims=True)
    e = jnp.exp(logits - top)
    lower = e[0:1, :] / jnp.sum(e, axis=0, keepdims=True)

    row = lax.broadcasted_iota(jnp.int32, (HG_TS, HG_TS), 0)
    col = lax.broadcasted_iota(jnp.int32, (HG_TS, HG_TS), 1)
    causal = (col <= row) & (col // HG_CHUNK == row // HG_CHUNK)
    tril = jnp.where(causal, 1.0, 0.0).astype(BF16)

    g = g_ref[...]
    hn = [_rms(h_ref[t], g).astype(BF16) for t in seqs]

    forget = [lower + (1.0 - lower) * _sigmoid(_dot(hn[t], win_ref[:, HG_FD:2 * HG_FD]))
              for t in seqs]
    k = [1.0 - forget[t] for t in seqs]
    log_f = [jnp.log(forget[t]) for t in seqs]
    hi = [log_f[t].astype(BF16) for t in seqs]
    lo = [(log_f[t] - hi[t].astype(F32)).astype(BF16) for t in seqs]
    b = [_dot(tril, hi[t]) + _dot(tril, lo[t]) for t in seqs]
    q = [_dot(hn[t], win_ref[:, 0:HG_FD]) for t in seqs]
    for t in seqs:
        qe_ref[t] = (q[t] * _sigmoid(q[t]) * jnp.exp(b[t])).astype(BF16)
        ke_ref[t] = (k[t] * jnp.exp(-b[t])).astype(BF16)
    for t in seqs:
        v_ref[t] = _dot(hn[t], win_ref[:, 2 * HG_FD:2 * HG_FD + D_MODEL]).astype(BF16)
    decay = [[] for _ in seqs]
    for t in seqs:
        for c in range(n_chunks):
            rs = slice(c * HG_CHUNK, (c + 1) * HG_CHUNK)
            b_end = b[t][(c + 1) * HG_CHUNK - 1:(c + 1) * HG_CHUNK, :]
            kd_ref[t, rs, :] = (k[t][rs, :] * jnp.exp(b_end - b[t][rs, :])).astype(BF16)
            decay[t].append(jnp.exp(b_end))
    for t in seqs:
        gate = _dot(hn[t], win_ref[:, 2 * HG_FD + D_MODEL:])
        gate_ref[t] = gate * _sigmoid(gate)

    og = og_ref[...]
    units = [(t, hd) for hd in range(HG_HEADS) for t in seqs]
    ks = [slice(hd * HG_DK, (hd + 1) * HG_DK) for hd in range(HG_HEADS)]
    vs = [slice(hd * HG_DV, (hd + 1) * HG_DV) for hd in range(HG_HEADS)]
    scores = {u: jnp.where(causal, _dot_nt(qe_ref[u[0], :, ks[u[1]]], ke_ref[u[0], :, ks[u[1]]]),
                           0.0).astype(BF16) for u in units}
    o_intra = {u: _dot(scores[u], v_ref[u[0], :, vs[u[1]]]) for u in units}
    st = {u: state_ref[u[0], u[1]] for u in units}
    o_parts = {u: [] for u in units}
    for c in range(n_chunks):
        rs = slice(c * HG_CHUNK, (c + 1) * HG_CHUNK)
        for u in units:
            t, hd = u
            o_parts[u].append(o_intra[u][rs, :]
                              + _dot_nt(qe_ref[t, rs, ks[hd]], st[u].astype(BF16)))
        for u in units:
            t, hd = u
            st[u] = (decay[t][c][:, ks[hd]] * st[u]
                     + _dot_tn(v_ref[t, rs, vs[hd]], kd_ref[t, rs, ks[hd]]))
    for u in units:
        t, hd = u
        state_ref[t, hd] = st[u]
        o = jnp.concatenate(o_parts[u], axis=0)
        y_ref[t, :, vs[hd]] = (_rms(o, og) * gate_ref[t, :, vs[hd]]).astype(BF16)

    for t in seqs:
        o_ref[t] = h_ref[t] + _dot(y_ref[t], wout_ref[...])


def _hgrn(h, g, w_in, lb_logits, onorm_g, w_out):
    batch, seq, _ = h.shape
    steps = seq // HG_TS
    n_lb = lb_logits.shape[0]
    tile = pl.BlockSpec((HG_SEQS, HG_TS, D_MODEL), lambda b, s: (b, s, 0))
    return pl.pallas_call(
        _hgrn_kernel,
        out_shape=jax.ShapeDtypeStruct(h.shape, F32),
        grid=(batch // HG_SEQS, steps),
        in_specs=[
            tile,
            _resident((1, D_MODEL)),
            _resident((D_MODEL, 2 * HG_FD + 2 * D_MODEL)),
            _resident((n_lb, HG_FD)),
            _resident((1, HG_DV)),
            _resident((D_MODEL, D_MODEL)),
        ],
        out_specs=tile,
        scratch_shapes=[
            pltpu.VMEM((HG_SEQS, HG_TS, HG_FD), BF16),
            pltpu.VMEM((HG_SEQS, HG_TS, HG_FD), BF16),
            pltpu.VMEM((HG_SEQS, HG_TS, HG_FD), BF16),
            pltpu.VMEM((HG_SEQS, HG_TS, D_MODEL), BF16),
            pltpu.VMEM((HG_SEQS, HG_TS, D_MODEL), F32),
            pltpu.VMEM((HG_SEQS, HG_TS, D_MODEL), BF16),
            pltpu.VMEM((HG_SEQS, HG_HEADS, HG_DV, HG_DK), F32),
        ],
        compiler_params=pltpu.CompilerParams(
            dimension_semantics=("parallel", "arbitrary"), vmem_limit_bytes=VMEM_LIMIT),
        name="hgrn",
    )(h, g, w_in, lb_logits, onorm_g, w_out)


def _swa_kernel(sink_ref, h_ref, g_ref, wq_ref, qg_ref, cos_ref, sin_ref,
                halo_ref, kv_ref, wo_ref, o_ref, q_buf, kvar, vvar, o_buf):
    x = h_ref[...]
    hn = _rms(x, g_ref[...]).astype(BF16)
    q = _dot(hn, wq_ref[...])
    wide = 2 * LANES
    ones = _group_ones(wide, ATT_HEAD_DIM)
    src = lax.broadcasted_iota(jnp.int32, (wide, wide), 0)
    dst = lax.broadcasted_iota(jnp.int32, (wide, wide), 1)
    dim = dst % ATT_HEAD_DIM
    half_rot = ROT_DIM // 2
    partner = jnp.where((dim < half_rot) & (src == dst + half_rot), -1.0,
                        jnp.where((dim >= half_rot) & (dim < ROT_DIM) & (src == dst - half_rot),
                                  1.0, 0.0)).astype(BF16)
    qg = jnp.concatenate([qg_ref[...]] * 2, axis=1)
    cos_t = jnp.concatenate([cos_ref[...]] * 2, axis=1)
    sin_t = jnp.concatenate([sin_ref[...]] * 2, axis=1)
    for s2 in range(Q_SLABS // 2):
        qq = q[:, s2 * wide:(s2 + 1) * wide]
        ms = _dot((qq * qq).astype(BF16), ones) * (1.0 / ATT_HEAD_DIM)
        qn = qq * lax.rsqrt(ms + EPS) * qg
        qs = qn * cos_t + _dot(qn.astype(BF16), partner) * sin_t
        q_buf[:, s2 * wide:(s2 + 1) * wide] = (qs * (ATT_SCALE * LOG2E)).astype(BF16)

    kv_all = jnp.concatenate([halo_ref[...], kv_ref[...]], axis=0)
    k_plain, k_swap = kv_all[:, 0:LANES], kv_all[:, LANES:2 * LANES]
    v_plain, v_swap = kv_all[:, 2 * LANES:3 * LANES], kv_all[:, 3 * LANES:]
    low = lax.broadcasted_iota(jnp.int32, k_plain.shape, 1) < ATT_HEAD_DIM
    zero = jnp.zeros_like(k_plain)
    ones_lo = jnp.where(low, 1.0, 0.0).astype(BF16)
    ones_hi = jnp.where(low, 0.0, 1.0).astype(BF16)
    k_src = ((k_plain, True), (k_swap, False), (k_swap, True), (k_plain, False))
    v_src = ((v_plain, True), (v_swap, False), (v_swap, True), (v_plain, False))
    for i, ((kt, in_low), (vt, _)) in enumerate(zip(k_src, v_src)):
        kvar[:, i * LANES:(i + 1) * LANES] = jnp.where(low, kt, zero) if in_low else jnp.where(low, zero, kt)
        vv = jnp.where(low, vt, zero) if in_low else jnp.where(low, zero, vt)
        vvar[:, 2 * i * LANES:2 * (i + 1) * LANES] = jnp.concatenate(
            [vv, ones_lo if in_low else ones_hi], axis=1)

    cur_ok = (lax.broadcasted_iota(jnp.int32, (WINDOW, WINDOW), 1)
              <= lax.broadcasted_iota(jnp.int32, (WINDOW, WINDOW), 0))
    low_q = lax.broadcasted_iota(jnp.int32, (WINDOW, LANES), 1) < ATT_HEAD_DIM
    at_seq_start = pl.program_id(1) == 0

    def block(qb, carry):
        r0 = pl.multiple_of(qb * WINDOW, WINDOW)
        rows = pl.ds(r0, WINDOW)
        band = pl.ds(r0, 2 * WINDOW)
        prev_bias = jnp.where(jnp.logical_and(at_seq_start, qb == 0), MASKED, 0.0)
        q_st = [jnp.concatenate(
            [q_buf[rows, (kvh * SLABS_PER_KV + p) * LANES:(kvh * SLABS_PER_KV + p + 1) * LANES]
             for p in range(SLABS_PER_KV)], axis=0) for kvh in range(ATT_KV_HEADS)]
        acc = [None] * ATT_KV_HEADS
        sink_e = [[] for _ in range(ATT_KV_HEADS)]
        for par in range(HEADS_PER_SLAB):
            s_all = [_dot_nt(q_st[kvh], kvar[band, (kvh * HEADS_PER_SLAB + par) * LANES:
                                             (kvh * HEADS_PER_SLAB + par + 1) * LANES])
                     for kvh in range(ATT_KV_HEADS)]
            for kvh in range(ATT_KV_HEADS):
                slab = kvh * HEADS_PER_SLAB + par
                es = []
                se = []
                for p in range(SLABS_PER_KV):
                    head = kvh * ATT_GROUP + p * HEADS_PER_SLAB + par
                    sink = sink_ref[head] * LOG2E
                    pr = slice(p * WINDOW, (p + 1) * WINDOW)
                    z = jnp.where(cur_ok, s_all[kvh][pr, WINDOW:], s_all[kvh][pr, 0:WINDOW] + prev_bias)
                    m = jnp.maximum(jnp.max(z, axis=1, keepdims=True), sink)
                    ex = jnp.exp2(z - m)
                    es.append(jnp.concatenate(
                        [jnp.where(cur_ok, 0.0, ex), jnp.where(cur_ok, ex, 0.0)],
                        axis=1).astype(BF16))
                    se.append(jnp.exp2(sink - m))
                e_all = jnp.concatenate(es, axis=0)
                part = _dot(e_all, vvar[band, 2 * slab * LANES:2 * (slab + 1) * LANES])
                acc[kvh] = part if acc[kvh] is None else acc[kvh] + part
                sink_e[kvh].append(se)
        for kvh in range(ATT_KV_HEADS):
            for p in range(SLABS_PER_KV):
                pr = slice(p * WINDOW, (p + 1) * WINDOW)
                den = acc[kvh][pr, LANES:] + jnp.where(low_q, sink_e[kvh][0][p], sink_e[kvh][1][p])
                sl = (kvh * SLABS_PER_KV + p) * LANES
                o_buf[rows, sl:sl + LANES] = (acc[kvh][pr, :LANES] / den).astype(BF16)
        return carry

    lax.fori_loop(0, ATT_TQ // WINDOW, block, 0, unroll=True)
    o_ref[...] = x + _dot(o_buf[...], wo_ref[...])


def _swa(h, g, w_q, qg, sinks, cos_t, sin_t, kv, w_o, batch, seq):
    steps = seq // ATT_TQ
    blocks_per_step = ATT_TQ // WINDOW
    blocks_per_seq = seq // WINDOW

    def halo_map(b, s):
        return (jnp.maximum(b * blocks_per_seq + s * blocks_per_step - 1, 0), 0)

    return pl.pallas_call(
        _swa_kernel,
        out_shape=jax.ShapeDtypeStruct(h.shape, F32),
        grid=(batch, steps),
        in_specs=[
            pl.BlockSpec(memory_space=pltpu.SMEM),
            pl.BlockSpec((ATT_TQ, D_MODEL), lambda b, s: (b * steps + s, 0)),
            _resident((1, D_MODEL)),
            _resident((D_MODEL, D_MODEL)),
            _resident((1, LANES)),
            pl.BlockSpec((ATT_TQ, LANES), lambda b, s: (s, 0)),
            pl.BlockSpec((ATT_TQ, LANES), lambda b, s: (s, 0)),
            pl.BlockSpec((WINDOW, KV_LANES), halo_map),
            pl.BlockSpec((ATT_TQ, KV_LANES), lambda b, s: (b * steps + s, 0)),
            _resident((D_MODEL, D_MODEL)),
        ],
        out_specs=pl.BlockSpec((ATT_TQ, D_MODEL), lambda b, s: (b * steps + s, 0)),
        scratch_shapes=[
            pltpu.VMEM((ATT_TQ, D_MODEL), BF16),
            pltpu.VMEM((ATT_TQ + WINDOW, 4 * LANES), BF16),
            pltpu.VMEM((ATT_TQ + WINDOW, 8 * LANES), BF16),
            pltpu.VMEM((ATT_TQ, D_MODEL), BF16),
        ],
        compiler_params=pltpu.CompilerParams(
            dimension_semantics=("parallel", "parallel"), vmem_limit_bytes=VMEM_LIMIT),
        name="swa",
    )(sinks, h, g, w_q, qg, cos_t, sin_t, kv, kv, w_o)


def _rope_tables(seq):
    half = ROT_DIM // 2
    inv_freq = jnp.power(ROPE_THETA, -jnp.arange(half, dtype=F32) * (2.0 / ROT_DIM))
    ang = jnp.arange(seq).astype(F32)[:, None] * inv_freq[None, :]
    cos, sin = jnp.cos(ang), jnp.sin(ang)
    zeros = jnp.zeros_like(sin)
    rest = ATT_HEAD_DIM - ROT_DIM
    pad0 = jnp.zeros((seq, rest), F32)
    cos_t = jnp.concatenate([cos, cos, jnp.ones((seq, rest), F32)], axis=1)
    sin_lo = jnp.concatenate([-sin, zeros, pad0], axis=1)
    sin_hi = jnp.concatenate([zeros, sin, pad0], axis=1)
    sin_t = jnp.concatenate([sin, sin, pad0], axis=1)
    return tuple(jnp.tile(t, (1, HEADS_PER_SLAB)) for t in (cos_t, sin_lo, sin_hi, sin_t))


def kernel(x, ffn_norm_g, ffn_w_gate_up, ffn_w_down, mix_norm_g, hgrn_w_in, hgrn_lb_logits,
           hgrn_onorm_g, hgrn_w_out, kv_norm_g, kv_w, k_norm_g, attn_w_q, q_norm_g,
           attn_sinks, attn_w_out):
    batch, seq, d = x.shape
    assert d == D_MODEL and seq % ATT_TQ == 0 and seq % HG_TS == 0 and batch % HG_SEQS == 0
    assert (batch * seq) % FFN_TM == 0 and seq % FFN_TM == 0
    h = x.reshape(batch * seq, d)

    cos_t, sin_lo, sin_hi, sin_t = _rope_tables(seq)
    ffn_g = ffn_norm_g.reshape(ffn_norm_g.shape[0], 2, 1, D_MODEL)

    h, w_gu01, w_d01, w_in, w_hout = _ffn(
        h, ffn_g[0, 0], ffn_w_gate_up[0, 0].astype(BF16), ffn_w_down[0, 0].astype(BF16), "ffn_0_0",
        casts=((ffn_w_gate_up, (0, 1)), (ffn_w_down, (0, 1)), (hgrn_w_in, (0,)), (hgrn_w_out, (0,))))
    h = _hgrn(h.reshape(batch, seq, d), mix_norm_g[0].reshape(1, D_MODEL), w_in,
              hgrn_lb_logits, hgrn_onorm_g[0].reshape(1, HG_DV), w_hout).reshape(batch * seq, d)
    h, w_gu10, w_d10, w_kv, w_q, w_aout = _ffn(
        h, ffn_g[0, 1], w_gu01, w_d01, "ffn_0_1",
        casts=((ffn_w_gate_up, (1, 0)), (ffn_w_down, (1, 0)), (kv_w, ()), (attn_w_q, (0,)),
               (attn_w_out, (0,))))
    h, kv, w_gu11, w_d11 = _ffn(
        h, ffn_g[1, 0], w_gu10, w_d10, "ffn_1_0",
        kv_args=(kv_norm_g.reshape(1, D_MODEL), w_kv,
                 jnp.tile(k_norm_g, HEADS_PER_SLAB).reshape(1, LANES),
                 cos_t, sin_lo, sin_hi, seq),
        casts=((ffn_w_gate_up, (1, 1)), (ffn_w_down, (1, 1))))
    h = _swa(h, mix_norm_g[1].reshape(1, D_MODEL), w_q,
             jnp.tile(q_norm_g[0], HEADS_PER_SLAB).reshape(1, LANES), attn_sinks[0],
             cos_t, sin_t, kv, w_aout, batch, seq)
    (h,) = _ffn(h, ffn_g[1, 1], w_gu11, w_d11, "ffn_1_1")
    return h.reshape(batch, seq, d)
```

```python
import functools

import jax
import jax.numpy as jnp
from jax import lax
from jax.experimental import pallas as pl
from jax.experimental.pallas import tpu as pltpu

D_MODEL = 1024
D_FF = 2816
MACARON_WEIGHT = 0.5

HG_HEADS = 8
HG_DK = 128
HG_FD = HG_HEADS * HG_DK
HG_DV = D_MODEL // HG_HEADS
HG_CHUNK = 64

ATT_HEADS = 16
ATT_KV_HEADS = 2
ATT_GROUP = ATT_HEADS // ATT_KV_HEADS
ATT_HEAD_DIM = 64
WINDOW = 128
ATT_SCALE = ATT_HEAD_DIM ** -0.5
ROPE_THETA = 500000.0
ROT_DIM = ATT_HEAD_DIM // 4
EPS = 1e-6

LANES = 128
BF16_SUBLANES = 16
HEADS_PER_SLAB = LANES // ATT_HEAD_DIM
Q_SLABS = ATT_HEADS // HEADS_PER_SLAB
SLABS_PER_KV = Q_SLABS // ATT_KV_HEADS
MASKED = -1e30
LOG2E = 1.4426950408889634
KV_LANES = 4 * LANES

FFN_TM = 1024
FFN_SPLIT = 2
FFN_CF = 256
HG_TS = 256
HG_SEQS = 4
ATT_TQ = 512
VMEM_LIMIT = 56 * 1024 * 1024

BF16 = jnp.bfloat16
F32 = jnp.float32
_DONE = object()


def _dot(a, b):
    return jnp.dot(a, b, preferred_element_type=F32)


def _dot_nt(a, b):
    return lax.dot_general(a, b, (((1,), (1,)), ((), ())), preferred_element_type=F32)


def _dot_tn(a, b):
    return lax.dot_general(a, b, (((0,), (0,)), ((), ())), preferred_element_type=F32)


def _rms(x, g):
    ms = jnp.mean(x * x, axis=-1, keepdims=True)
    return x * lax.rsqrt(ms + EPS) * g


def _sigmoid(x):
    return 1.0 / (1.0 + jnp.exp(-x))


def _group_mean_sq(x, group_ones, inv_n):
    x2 = x * x
    hi = x2.astype(BF16)
    lo = (x2 - hi.astype(F32)).astype(BF16)
    return (_dot(hi, group_ones) + _dot(lo, group_ones)) * inv_n


def _group_ones(n, group):
    r = lax.broadcasted_iota(jnp.int32, (n, n), 0) // group
    c = lax.broadcasted_iota(jnp.int32, (n, n), 1) // group
    return jnp.where(r == c, 1.0, 0.0).astype(BF16)


def _rope_slab(x, cos_t, sin_lo, sin_hi):
    return (x * cos_t + pltpu.roll(x, LANES - ROT_DIM // 2, axis=1) * sin_lo
            + pltpu.roll(x, ROT_DIM // 2, axis=1) * sin_hi)


def _resident(shape):
    nd = len(shape)
    return pl.BlockSpec(shape, lambda *_: (0,) * nd, pipeline_mode=pl.Buffered(1))


def _swiglu_residual(x_ref, g, wgu_ref, wd_ref, o_ref):
    rows = FFN_TM // FFN_SPLIT
    groups = [slice(i * rows, (i + 1) * rows) for i in range(FFN_SPLIT)]
    hns = [_rms(x_ref[r, :], g).astype(BF16) for r in groups]
    accs = [None] * FFN_SPLIT
    for c in range(D_FF // FFN_CF):
        lo = c * FFN_CF
        for i, hn in enumerate(hns):
            gate = _dot(hn, wgu_ref[:, lo:lo + FFN_CF])
            up = _dot(hn, wgu_ref[:, D_FF + lo:D_FF + lo + FFN_CF])
            act = (gate * _sigmoid(gate) * up).astype(BF16)
            part = _dot(act, wd_ref[lo:lo + FFN_CF, :])
            accs[i] = part if accs[i] is None else accs[i] + part
    for r, acc in zip(groups, accs):
        o_ref[r, :] = x_ref[r, :] + MACARON_WEIGHT * acc


def _shared_kv(x_ref, kvg_ref, kvw_ref, kg_ref, cos_ref, slo_ref, shi_ref, kv_ref):
    rows = FFN_TM // FFN_SPLIT
    ones = _group_ones(LANES, ATT_HEAD_DIM)
    for i in range(FFN_SPLIT):
        r = slice(i * rows, (i + 1) * rows)
        hn = _rms(x_ref[r, :], kvg_ref[...]).astype(BF16)
        kv = _dot(hn, kvw_ref[...])
        k = kv[:, :LANES]
        v = kv[:, LANES:]
        ms = _group_mean_sq(k, ones, 1.0 / ATT_HEAD_DIM)
        k = k * lax.rsqrt(ms + EPS) * kg_ref[...]
        k = _rope_slab(k, cos_ref[r, :], slo_ref[r, :], shi_ref[r, :])
        kv_ref[r, 0 * LANES:1 * LANES] = k.astype(BF16)
        kv_ref[r, 1 * LANES:2 * LANES] = pltpu.roll(k, ATT_HEAD_DIM, axis=1).astype(BF16)
        kv_ref[r, 2 * LANES:3 * LANES] = v.astype(BF16)
        kv_ref[r, 3 * LANES:4 * LANES] = pltpu.roll(v, ATT_HEAD_DIM, axis=1).astype(BF16)


def _ffn_kernel(*refs, with_kv, n_cast):
    n_in = 4 + (6 if with_kv else 0) + n_cast
    ins, outs = refs[:n_in], refs[n_in:]
    x_ref, g_ref, wgu_ref, wd_ref = ins[:4]
    for src_ref, dst_ref in zip(ins[n_in - n_cast:], outs[len(outs) - n_cast:]):
        dst_ref[...] = src_ref[...].astype(BF16)
    if with_kv:
        _shared_kv(x_ref, *ins[4:10], outs[1])
    _swiglu_residual(x_ref, g_ref[...], wgu_ref, wd_ref, outs[0])


def _cast_rows(n_rows, steps):
    rows = BF16_SUBLANES
    while n_rows % rows or n_rows // rows > steps:
        rows += BF16_SUBLANES
    return rows


def _ffn(h, g, w_gu, w_d, name, kv_args=None, casts=()):
    t = h.shape[0]
    steps = t // FFN_TM
    in_specs = [
        pl.BlockSpec((FFN_TM, D_MODEL), lambda i: (i, 0)),
        _resident((1, D_MODEL)),
        _resident((D_MODEL, 2 * D_FF)),
        _resident((D_FF, D_MODEL)),
    ]
    out_shape = [jax.ShapeDtypeStruct((t, D_MODEL), F32)]
    out_specs = [pl.BlockSpec((FFN_TM, D_MODEL), lambda i: (i, 0))]
    args = [h, g, w_gu, w_d]
    if kv_args is not None:
        seq = kv_args[-1]
        per_seq = seq // FFN_TM
        table = pl.BlockSpec((FFN_TM, LANES), lambda i: (i % per_seq, 0))
        in_specs += [_resident((1, D_MODEL)), _resident((D_MODEL, 2 * LANES)),
                     _resident((1, LANES)), table, table, table]
        out_shape.append(jax.ShapeDtypeStruct((t, KV_LANES), BF16))
        out_specs.append(pl.BlockSpec((FFN_TM, KV_LANES), lambda i: (i, 0)))
        args += list(kv_args[:-1])
    for w, lead in casts:
        n_rows, cols = w.shape[len(lead):]
        rows = _cast_rows(n_rows, steps)
        last = n_rows // rows - 1
        in_specs.append(pl.BlockSpec(
            (None,) * len(lead) + (rows, cols),
            lambda i, lead=lead, last=last: lead + (jnp.minimum(i, last), 0)))
        out_specs.append(pl.BlockSpec((rows, cols), lambda i, last=last: (jnp.minimum(i, last), 0)))
        out_shape.append(jax.ShapeDtypeStruct((n_rows, cols), BF16))
        args.append(w)
    return pl.pallas_call(
        functools.partial(_ffn_kernel, with_kv=kv_args is not None, n_cast=len(casts)),
        out_shape=tuple(out_shape),
        grid=(steps,),
        in_specs=in_specs,
        out_specs=tuple(out_specs),
        compiler_params=pltpu.CompilerParams(
            dimension_semantics=("arbitrary",), vmem_limit_bytes=VMEM_LIMIT),
        name=name,
    )(*args)


def _hgrn_kernel(h_ref, g_ref, win_ref, lbl_ref, og_ref, wout_ref, o_ref,
                 qe_ref, ke_ref, kd_ref, v_ref, gate_ref, y_ref, state_ref):
    n_chunks = HG_TS // HG_CHUNK
    seqs = range(HG_SEQS)

    @pl.when(pl.program_id(1) == 0)
    def _():
        state_ref[...] = jnp.zeros_like(state_ref)

    logits = lbl_ref[...]
    top = jnp.max(logits, axis=0, keepdims=True)
    e = jnp.exp(logits - top)
    lower = e[0:1, :] / jnp.sum(e, axis=0, keepdims=True)

    row = lax.broadcasted_iota(jnp.int32, (HG_TS, HG_TS), 0)
    col = lax.broadcasted_iota(jnp.int32, (HG_TS, HG_TS), 1)
    causal = (col <= row) & (col // HG_CHUNK == row // HG_CHUNK)
    tril = jnp.where(causal, 1.0, 0.0).astype(BF16)

    g = g_ref[...]
    hn = [_rms(h_ref[t], g).astype(BF16) for t in seqs]

    forget = [lower + (1.0 - lower) * _sigmoid(_dot(hn[t], win_ref[:, HG_FD:2 * HG_FD]))
              for t in seqs]
    k = [1.0 - forget[t] for t in seqs]
    log_f = [jnp.log(forget[t]) for t in seqs]
    hi = [log_f[t].astype(BF16) for t in seqs]
    lo = [(log_f[t] - hi[t].astype(F32)).astype(BF16) for t in seqs]
    b = [_dot(tril, hi[t]) + _dot(tril, lo[t]) for t in seqs]
    q = [_dot(hn[t], win_ref[:, 0:HG_FD]) for t in seqs]
    for t in seqs:
        qe_ref[t] = (q[t] * _sigmoid(q[t]) * jnp.exp(b[t])).astype(BF16)
        ke_ref[t] = (k[t] * jnp.exp(-b[t])).astype(BF16)
    for t in seqs:
        v_ref[t] = _dot(hn[t], win_ref[:, 2 * HG_FD:2 * HG_FD + D_MODEL]).astype(BF16)
    decay = [[] for _ in seqs]
    for t in seqs:
        for c in range(n_chunks):
            rs = slice(c * HG_CHUNK, (c + 1) * HG_CHUNK)
            b_end = b[t][(c + 1) * HG_CHUNK - 1:(c + 1) * HG_CHUNK, :]
            kd_ref[t, rs, :] = (k[t][rs, :] * jnp.exp(b_end - b[t][rs, :])).astype(BF16)
            decay[t].append(jnp.exp(b_end))
    for t in seqs:
        gate = _dot(hn[t], win_ref[:, 2 * HG_FD + D_MODEL:])
        gate_ref[t] = gate * _sigmoid(gate)

    og = og_ref[...]
    heads = range(HG_HEADS)
    ks = [slice(hd * HG_DK, (hd + 1) * HG_DK) for hd in heads]
    vs = [slice(hd * HG_DV, (hd + 1) * HG_DV) for hd in heads]
    pad = jnp.zeros((HG_CHUNK, HG_DK), BF16)
    o_intra = [[None] * HG_HEADS for _ in seqs]
    incr = [[[None] * n_chunks for _ in heads] for _ in seqs]

    def within_chunks(t):
        scores = []
        for hd in heads:
            scores.append(jnp.where(causal, _dot_nt(qe_ref[t, :, ks[hd]], ke_ref[t, :, ks[hd]]),
                                    0.0).astype(BF16))
            yield
        for hd in heads:
            o_intra[t][hd] = _dot(scores[hd], v_ref[t, :, vs[hd]])
            yield
        for c in range(0, n_chunks, 2):
            r0 = slice(c * HG_CHUNK, (c + 1) * HG_CHUNK)
            r1 = slice((c + 1) * HG_CHUNK, (c + 2) * HG_CHUNK)
            both = slice(c * HG_CHUNK, (c + 2) * HG_CHUNK)
            for hd in heads:
                kd_pair = jnp.concatenate(
                    [jnp.concatenate([kd_ref[t, r0, ks[hd]], pad], axis=1),
                     jnp.concatenate([pad, kd_ref[t, r1, ks[hd]]], axis=1)], axis=0)
                pair = _dot_tn(v_ref[t, both, vs[hd]], kd_pair)
                incr[t][hd][c] = pair[:, :HG_DK]
                incr[t][hd][c + 1] = pair[:, HG_DK:]
                yield

    def across_chunks(t):
        st = [state_ref[t, hd] for hd in heads]
        o_parts = [[] for _ in heads]
        for c in range(n_chunks):
            rs = slice(c * HG_CHUNK, (c + 1) * HG_CHUNK)
            for hd in heads:
                o_parts[hd].append(o_intra[t][hd][rs, :]
                                   + _dot_nt(qe_ref[t, rs, ks[hd]], st[hd].astype(BF16)))
                yield
            for hd in heads:
                st[hd] = decay[t][c][:, ks[hd]] * st[hd] + incr[t][hd][c]
        for hd in heads:
            state_ref[t, hd] = st[hd]
            o = jnp.concatenate(o_parts[hd], axis=0)
            y_ref[t, :, vs[hd]] = (_rms(o, og) * gate_ref[t, :, vs[hd]]).astype(BF16)
            yield

    def project_out(t):
        o_ref[t] = h_ref[t] + _dot(y_ref[t], wout_ref[...])
        yield

    def emit_together(*stages):
        live = list(stages)
        while live:
            for stage in list(live):
                if next(stage, _DONE) is _DONE:
                    live.remove(stage)

    emit_together(within_chunks(0))
    for t in range(1, HG_SEQS + 2):
        emit_together(*[stage(t - lag) for lag, stage in
                        enumerate((within_chunks, across_chunks, project_out))
                        if 0 <= t - lag < HG_SEQS])


def _hgrn(h, g, w_in, lb_logits, onorm_g, w_out):
    batch, seq, _ = h.shape
    steps = seq // HG_TS
    n_lb = lb_logits.shape[0]
    tile = pl.BlockSpec((HG_SEQS, HG_TS, D_MODEL), lambda b, s: (b, s, 0))
    return pl.pallas_call(
        _hgrn_kernel,
        out_shape=jax.ShapeDtypeStruct(h.shape, F32),
        grid=(batch // HG_SEQS, steps),
        in_specs=[
            tile,
            _resident((1, D_MODEL)),
            _resident((D_MODEL, 2 * HG_FD + 2 * D_MODEL)),
            _resident((n_lb, HG_FD)),
            _resident((1, HG_DV)),
            _resident((D_MODEL, D_MODEL)),
        ],
        out_specs=tile,
        scratch_shapes=[
            pltpu.VMEM((HG_SEQS, HG_TS, HG_FD), BF16),
            pltpu.VMEM((HG_SEQS, HG_TS, HG_FD), BF16),
            pltpu.VMEM((HG_SEQS, HG_TS, HG_FD), BF16),
            pltpu.VMEM((HG_SEQS, HG_TS, D_MODEL), BF16),
            pltpu.VMEM((HG_SEQS, HG_TS, D_MODEL), F32),
            pltpu.VMEM((HG_SEQS, HG_TS, D_MODEL), BF16),
            pltpu.VMEM((HG_SEQS, HG_HEADS, HG_DV, HG_DK), F32),
        ],
        compiler_params=pltpu.CompilerParams(
            dimension_semantics=("parallel", "arbitrary"), vmem_limit_bytes=VMEM_LIMIT),
        name="hgrn",
    )(h, g, w_in, lb_logits, onorm_g, w_out)


def _swa_kernel(sink_ref, h_ref, g_ref, wq_ref, qg_ref, cos_ref, sin_ref,
                halo_ref, kv_ref, wo_ref, o_ref, q_buf, kvar, vvar, o_buf):
    x = h_ref[...]
    hn = _rms(x, g_ref[...]).astype(BF16)
    q = _dot(hn, wq_ref[...])
    wide = 2 * LANES
    ones = _group_ones(wide, ATT_HEAD_DIM)
    src = lax.broadcasted_iota(jnp.int32, (wide, wide), 0)
    dst = lax.broadcasted_iota(jnp.int32, (wide, wide), 1)
    dim = dst % ATT_HEAD_DIM
    half_rot = ROT_DIM // 2
    partner = jnp.where((dim < half_rot) & (src == dst + half_rot), -1.0,
                        jnp.where((dim >= half_rot) & (dim < ROT_DIM) & (src == dst - half_rot),
                                  1.0, 0.0)).astype(BF16)
    qg = jnp.concatenate([qg_ref[...]] * 2, axis=1)
    cos_t = jnp.concatenate([cos_ref[...]] * 2, axis=1)
    sin_t = jnp.concatenate([sin_ref[...]] * 2, axis=1)
    for s2 in range(Q_SLABS // 2):
        qq = q[:, s2 * wide:(s2 + 1) * wide]
        ms = _dot((qq * qq).astype(BF16), ones) * (1.0 / ATT_HEAD_DIM)
        qn = qq * lax.rsqrt(ms + EPS) * qg
        qs = qn * cos_t + _dot(qn.astype(BF16), partner) * sin_t
        q_buf[:, s2 * wide:(s2 + 1) * wide] = (qs * (ATT_SCALE * LOG2E)).astype(BF16)

    kv_all = jnp.concatenate([halo_ref[...], kv_ref[...]], axis=0)
    k_plain, k_swap = kv_all[:, 0:LANES], kv_all[:, LANES:2 * LANES]
    v_plain, v_swap = kv_all[:, 2 * LANES:3 * LANES], kv_all[:, 3 * LANES:]
    low = lax.broadcasted_iota(jnp.int32, k_plain.shape, 1) < ATT_HEAD_DIM
    zero = jnp.zeros_like(k_plain)
    ones_lo = jnp.where(low, 1.0, 0.0).astype(BF16)
    ones_hi = jnp.where(low, 0.0, 1.0).astype(BF16)
    k_src = ((k_plain, True), (k_swap, False), (k_swap, True), (k_plain, False))
    v_src = ((v_plain, True), (v_swap, False), (v_swap, True), (v_plain, False))
    for i, ((kt, in_low), (vt, _)) in enumerate(zip(k_src, v_src)):
        kvar[:, i * LANES:(i + 1) * LANES] = jnp.where(low, kt, zero) if in_low else jnp.where(low, zero, kt)
        vv = jnp.where(low, vt, zero) if in_low else jnp.where(low, zero, vt)
        vvar[:, 2 * i * LANES:2 * (i + 1) * LANES] = jnp.concatenate(
            [vv, ones_lo if in_low else ones_hi], axis=1)

    cur_ok = (lax.broadcasted_iota(jnp.int32, (WINDOW, WINDOW), 1)
              <= lax.broadcasted_iota(jnp.int32, (WINDOW, WINDOW), 0))
    low_q = lax.broadcasted_iota(jnp.int32, (WINDOW, LANES), 1) < ATT_HEAD_DIM
    at_seq_start = pl.program_id(1) == 0

    def block(qb, carry):
        r0 = pl.multiple_of(qb * WINDOW, WINDOW)
        rows = pl.ds(r0, WINDOW)
        band = pl.ds(r0, 2 * WINDOW)
        prev_bias = jnp.where(jnp.logical_and(at_seq_start, qb == 0), MASKED, 0.0)
        q_st = [jnp.concatenate(
            [q_buf[rows, (kvh * SLABS_PER_KV + p) * LANES:(kvh * SLABS_PER_KV + p + 1) * LANES]
             for p in range(SLABS_PER_KV)], axis=0) for kvh in range(ATT_KV_HEADS)]
        acc = [None] * ATT_KV_HEADS
        sink_e = [[] for _ in range(ATT_KV_HEADS)]
        for par in range(HEADS_PER_SLAB):
            s_all = [_dot_nt(q_st[kvh], kvar[band, (kvh * HEADS_PER_SLAB + par) * LANES:
                                             (kvh * HEADS_PER_SLAB + par + 1) * LANES])
                     for kvh in range(ATT_KV_HEADS)]
            for kvh in range(ATT_KV_HEADS):
                slab = kvh * HEADS_PER_SLAB + par
                es = []
                se = []
                for p in range(SLABS_PER_KV):
                    head = kvh * ATT_GROUP + p * HEADS_PER_SLAB + par
                    sink = sink_ref[head] * LOG2E
                    pr = slice(p * WINDOW, (p + 1) * WINDOW)
                    z = jnp.where(cur_ok, s_all[kvh][pr, WINDOW:], s_all[kvh][pr, 0:WINDOW] + prev_bias)
                    m = jnp.maximum(jnp.max(z, axis=1, keepdims=True), sink)
                    ex = jnp.exp2(z - m)
                    es.append(jnp.concatenate(
                        [jnp.where(cur_ok, 0.0, ex), jnp.where(cur_ok, ex, 0.0)],
                        axis=1).astype(BF16))
                    se.append(jnp.exp2(sink - m))
                e_all = jnp.concatenate(es, axis=0)
                part = _dot(e_all, vvar[band, 2 * slab * LANES:2 * (slab + 1) * LANES])
                acc[kvh] = part if acc[kvh] is None else acc[kvh] + part
                sink_e[kvh].append(se)
        for kvh in range(ATT_KV_HEADS):
            for p in range(SLABS_PER_KV):
                pr = slice(p * WINDOW, (p + 1) * WINDOW)
                den = acc[kvh][pr, LANES:] + jnp.where(low_q, sink_e[kvh][0][p], sink_e[kvh][1][p])
                sl = (kvh * SLABS_PER_KV + p) * LANES
                o_buf[rows, sl:sl + LANES] = (acc[kvh][pr, :LANES] / den).astype(BF16)
        return carry

    lax.fori_loop(0, ATT_TQ // WINDOW, block, 0, unroll=True)
    o_ref[...] = x + _dot(o_buf[...], wo_ref[...])


def _swa(h, g, w_q, qg, sinks, cos_t, sin_t, kv, w_o, batch, seq):
    steps = seq // ATT_TQ
    blocks_per_step = ATT_TQ // WINDOW
    blocks_per_seq = seq // WINDOW

    def halo_map(b, s):
        return (jnp.maximum(b * blocks_per_seq + s * blocks_per_step - 1, 0), 0)

    return pl.pallas_call(
        _swa_kernel,
        out_shape=jax.ShapeDtypeStruct(h.shape, F32),
        grid=(batch, steps),
        in_specs=[
            pl.BlockSpec(memory_space=pltpu.SMEM),
            pl.BlockSpec((ATT_TQ, D_MODEL), lambda b, s: (b * steps + s, 0)),
            _resident((1, D_MODEL)),
            _resident((D_MODEL, D_MODEL)),
            _resident((1, LANES)),
            pl.BlockSpec((ATT_TQ, LANES), lambda b, s: (s, 0)),
            pl.BlockSpec((ATT_TQ, LANES), lambda b, s: (s, 0)),
            pl.BlockSpec((WINDOW, KV_LANES), halo_map),
            pl.BlockSpec((ATT_TQ, KV_LANES), lambda b, s: (b * steps + s, 0)),
            _resident((D_MODEL, D_MODEL)),
        ],
        out_specs=pl.BlockSpec((ATT_TQ, D_MODEL), lambda b, s: (b * steps + s, 0)),
        scratch_shapes=[
            pltpu.VMEM((ATT_TQ, D_MODEL), BF16),
            pltpu.VMEM((ATT_TQ + WINDOW, 4 * LANES), BF16),
            pltpu.VMEM((ATT_TQ + WINDOW, 8 * LANES), BF16),
            pltpu.VMEM((ATT_TQ, D_MODEL), BF16),
        ],
        compiler_params=pltpu.CompilerParams(
            dimension_semantics=("parallel", "parallel"), vmem_limit_bytes=VMEM_LIMIT),
        name="swa",
    )(sinks, h, g, w_q, qg, cos_t, sin_t, kv, kv, w_o)


def _rope_tables(seq):
    half = ROT_DIM // 2
    inv_freq = jnp.power(ROPE_THETA, -jnp.arange(half, dtype=F32) * (2.0 / ROT_DIM))
    ang = jnp.arange(seq).astype(F32)[:, None] * inv_freq[None, :]
    cos, sin = jnp.cos(ang), jnp.sin(ang)
    zeros = jnp.zeros_like(sin)
    rest = ATT_HEAD_DIM - ROT_DIM
    pad0 = jnp.zeros((seq, rest), F32)
    cos_t = jnp.concatenate([cos, cos, jnp.ones((seq, rest), F32)], axis=1)
    sin_lo = jnp.concatenate([-sin, zeros, pad0], axis=1)
    sin_hi = jnp.concatenate([zeros, sin, pad0], axis=1)
    sin_t = jnp.concatenate([sin, sin, pad0], axis=1)
    return tuple(jnp.tile(t, (1, HEADS_PER_SLAB)) for t in (cos_t, sin_lo, sin_hi, sin_t))


def kernel(x, ffn_norm_g, ffn_w_gate_up, ffn_w_down, mix_norm_g, hgrn_w_in, hgrn_lb_logits,
           hgrn_onorm_g, hgrn_w_out, kv_norm_g, kv_w, k_norm_g, attn_w_q, q_norm_g,
           attn_sinks, attn_w_out):
    batch, seq, d = x.shape
    assert d == D_MODEL and seq % ATT_TQ == 0 and seq % HG_TS == 0 and batch % HG_SEQS == 0
    assert (batch * seq) % FFN_TM == 0 and seq % FFN_TM == 0
    h = x.reshape(batch * seq, d)

    cos_t, sin_lo, sin_hi, sin_t = _rope_tables(seq)
    ffn_g = ffn_norm_g.reshape(ffn_norm_g.shape[0], 2, 1, D_MODEL)

    h, w_gu01, w_d01, w_in, w_hout = _ffn(
        h, ffn_g[0, 0], ffn_w_gate_up[0, 0].astype(BF16), ffn_w_down[0, 0].astype(BF16), "ffn_0_0",
        casts=((ffn_w_gate_up, (0, 1)), (ffn_w_down, (0, 1)), (hgrn_w_in, (0,)), (hgrn_w_out, (0,))))
    h = _hgrn(h.reshape(batch, seq, d), mix_norm_g[0].reshape(1, D_MODEL), w_in,
              hgrn_lb_logits, hgrn_onorm_g[0].reshape(1, HG_DV), w_hout).reshape(batch * seq, d)
    h, w_gu10, w_d10, w_kv, w_q, w_aout = _ffn(
        h, ffn_g[0, 1], w_gu01, w_d01, "ffn_0_1",
        casts=((ffn_w_gate_up, (1, 0)), (ffn_w_down, (1, 0)), (kv_w, ()), (attn_w_q, (0,)),
               (attn_w_out, (0,))))
    h, kv, w_gu11, w_d11 = _ffn(
        h, ffn_g[1, 0], w_gu10, w_d10, "ffn_1_0",
        kv_args=(kv_norm_g.reshape(1, D_MODEL), w_kv,
                 jnp.tile(k_norm_g, HEADS_PER_SLAB).reshape(1, LANES),
                 cos_t, sin_lo, sin_hi, seq),
        casts=((ffn_w_gate_up, (1, 1)), (ffn_w_down, (1, 1))))
    h = _swa(h, mix_norm_g[1].reshape(1, D_MODEL), w_q,
             jnp.tile(q_norm_g[0], HEADS_PER_SLAB).reshape(1, LANES), attn_sinks[0],
             cos_t, sin_t, kv, w_aout, batch, seq)
    (h,) = _ffn(h, ffn_g[1, 1], w_gu11, w_d11, "ffn_1_1")
    return h.reshape(batch, seq, d)
```

```python
import functools

import jax
import jax.numpy as jnp
from jax import lax
from jax.experimental import pallas as pl
from jax.experimental.pallas import tpu as pltpu

D_MODEL = 1024
D_FF = 2816
MACARON_WEIGHT = 0.5

HG_HEADS = 8
HG_DK = 128
HG_FD = HG_HEADS * HG_DK
HG_DV = D_MODEL // HG_HEADS
HG_CHUNK = 64

ATT_HEADS = 16
ATT_KV_HEADS = 2
ATT_GROUP = ATT_HEADS // ATT_KV_HEADS
ATT_HEAD_DIM = 64
WINDOW = 128
ATT_SCALE = ATT_HEAD_DIM ** -0.5
ROPE_THETA = 500000.0
ROT_DIM = ATT_HEAD_DIM // 4
EPS = 1e-6

LANES = 128
BF16_SUBLANES = 16
HEADS_PER_SLAB = LANES // ATT_HEAD_DIM
Q_SLABS = ATT_HEADS // HEADS_PER_SLAB
SLABS_PER_KV = Q_SLABS // ATT_KV_HEADS
MASKED = -1e30
LOG2E = 1.4426950408889634
KV_LANES = 4 * LANES

FFN_TM = 1024
FFN_SPLIT = 2
FFN_CF = 256
HG_TS = 256
HG_SEQS = 4
ATT_TQ = 1024
VMEM_LIMIT = 56 * 1024 * 1024

BF16 = jnp.bfloat16
F32 = jnp.float32
_DONE = object()


def _dot(a, b):
    return jnp.dot(a, b, preferred_element_type=F32)


def _dot_nt(a, b):
    return lax.dot_general(a, b, (((1,), (1,)), ((), ())), preferred_element_type=F32)


def _dot_tn(a, b):
    return lax.dot_general(a, b, (((0,), (0,)), ((), ())), preferred_element_type=F32)


def _rms(x, g):
    ms = jnp.mean(x * x, axis=-1, keepdims=True)
    return x * lax.rsqrt(ms + EPS) * g


def _sigmoid(x):
    return 1.0 / (1.0 + jnp.exp(-x))


def _group_mean_sq(x, group_ones, inv_n):
    x2 = x * x
    hi = x2.astype(BF16)
    lo = (x2 - hi.astype(F32)).astype(BF16)
    return (_dot(hi, group_ones) + _dot(lo, group_ones)) * inv_n


def _group_ones(n, group):
    r = lax.broadcasted_iota(jnp.int32, (n, n), 0) // group
    c = lax.broadcasted_iota(jnp.int32, (n, n), 1) // group
    return jnp.where(r == c, 1.0, 0.0).astype(BF16)


def _rope_slab(x, cos_t, sin_lo, sin_hi):
    return (x * cos_t + pltpu.roll(x, LANES - ROT_DIM // 2, axis=1) * sin_lo
            + pltpu.roll(x, ROT_DIM // 2, axis=1) * sin_hi)


def _resident(shape):
    nd = len(shape)
    return pl.BlockSpec(shape, lambda *_: (0,) * nd, pipeline_mode=pl.Buffered(1))


def _inv_rms(x):
    return lax.rsqrt(jnp.mean(x * x, axis=-1, keepdims=True) + EPS)


def _swiglu_residual(x_ref, g, wgu_ref, wd_ref, o_ref, before=None):
    rows = FFN_TM // FFN_SPLIT
    groups = [slice(i * rows, (i + 1) * rows) for i in range(FFN_SPLIT)]
    inv = [_inv_rms(x_ref[r, :]) for r in groups]
    hns = [(x_ref[r, :] * inv[i] * g).astype(BF16) for i, r in enumerate(groups)]
    if before is not None:
        before(inv)
    accs = [None] * FFN_SPLIT
    for c in range(D_FF // FFN_CF):
        lo = c * FFN_CF
        for i, hn in enumerate(hns):
            gate = _dot(hn, wgu_ref[:, lo:lo + FFN_CF])
            up = _dot(hn, wgu_ref[:, D_FF + lo:D_FF + lo + FFN_CF])
            act = (gate * _sigmoid(gate) * up).astype(BF16)
            part = _dot(act, wd_ref[lo:lo + FFN_CF, :])
            accs[i] = part if accs[i] is None else accs[i] + part
    for r, acc in zip(groups, accs):
        o_ref[r, :] = x_ref[r, :] + MACARON_WEIGHT * acc


def _shared_kv(i, inv, x_ref, kvg_ref, kvw_ref, kg_ref, cos_ref, slo_ref, shi_ref, kv_ref):
    rows = FFN_TM // FFN_SPLIT
    r = slice(i * rows, (i + 1) * rows)
    hn = (x_ref[r, :] * inv * kvg_ref[...]).astype(BF16)
    kv = _dot(hn, kvw_ref[...])
    k = kv[:, :LANES]
    v = kv[:, LANES:]
    ms = _group_mean_sq(k, _group_ones(LANES, ATT_HEAD_DIM), 1.0 / ATT_HEAD_DIM)
    k = k * lax.rsqrt(ms + EPS) * kg_ref[...]
    k = _rope_slab(k, cos_ref[r, :], slo_ref[r, :], shi_ref[r, :])
    kv_ref[r, 0 * LANES:1 * LANES] = k.astype(BF16)
    kv_ref[r, 1 * LANES:2 * LANES] = pltpu.roll(k, ATT_HEAD_DIM, axis=1).astype(BF16)
    kv_ref[r, 2 * LANES:3 * LANES] = v.astype(BF16)
    kv_ref[r, 3 * LANES:4 * LANES] = pltpu.roll(v, ATT_HEAD_DIM, axis=1).astype(BF16)


def _ffn_kernel(*refs, with_kv, n_cast):
    n_in = 4 + (6 if with_kv else 0) + n_cast
    ins, outs = refs[:n_in], refs[n_in:]
    x_ref, g_ref, wgu_ref, wd_ref = ins[:4]
    for src_ref, dst_ref in zip(ins[n_in - n_cast:], outs[len(outs) - n_cast:]):
        dst_ref[...] = src_ref[...].astype(BF16)

    def kv_first(inv):
        for i in range(FFN_SPLIT):
            _shared_kv(i, inv[i], x_ref, *ins[4:10], outs[1])

    _swiglu_residual(x_ref, g_ref[...], wgu_ref, wd_ref, outs[0],
                     before=kv_first if with_kv else None)


def _cast_rows(n_rows, steps):
    rows = BF16_SUBLANES
    while n_rows % rows or n_rows // rows > steps:
        rows += BF16_SUBLANES
    return rows


def _ffn(h, g, w_gu, w_d, name, kv_args=None, casts=()):
    t = h.shape[0]
    steps = t // FFN_TM
    in_specs = [
        pl.BlockSpec((FFN_TM, D_MODEL), lambda i: (i, 0)),
        _resident((1, D_MODEL)),
        _resident((D_MODEL, 2 * D_FF)),
        _resident((D_FF, D_MODEL)),
    ]
    out_shape = [jax.ShapeDtypeStruct((t, D_MODEL), F32)]
    out_specs = [pl.BlockSpec((FFN_TM, D_MODEL), lambda i: (i, 0))]
    args = [h, g, w_gu, w_d]
    if kv_args is not None:
        seq = kv_args[-1]
        per_seq = seq // FFN_TM
        table = pl.BlockSpec((FFN_TM, LANES), lambda i: (i % per_seq, 0))
        in_specs += [_resident((1, D_MODEL)), _resident((D_MODEL, 2 * LANES)),
                     _resident((1, LANES)), table, table, table]
        out_shape.append(jax.ShapeDtypeStruct((t, KV_LANES), BF16))
        out_specs.append(pl.BlockSpec((FFN_TM, KV_LANES), lambda i: (i, 0)))
        args += list(kv_args[:-1])
    for w, lead in casts:
        n_rows, cols = w.shape[len(lead):]
        rows = _cast_rows(n_rows, steps)
        last = n_rows // rows - 1
        in_specs.append(pl.BlockSpec(
            (None,) * len(lead) + (rows, cols),
            lambda i, lead=lead, last=last: lead + (jnp.minimum(i, last), 0)))
        out_specs.append(pl.BlockSpec((rows, cols), lambda i, last=last: (jnp.minimum(i, last), 0)))
        out_shape.append(jax.ShapeDtypeStruct((n_rows, cols), BF16))
        args.append(w)
    return pl.pallas_call(
        functools.partial(_ffn_kernel, with_kv=kv_args is not None, n_cast=len(casts)),
        out_shape=tuple(out_shape),
        grid=(steps,),
        in_specs=in_specs,
        out_specs=tuple(out_specs),
        compiler_params=pltpu.CompilerParams(
            dimension_semantics=("arbitrary",), vmem_limit_bytes=VMEM_LIMIT),
        name=name,
    )(*args)


def _hgrn_kernel(h_ref, g_ref, win_ref, lbl_ref, og_ref, wout_ref, o_ref,
                 qe_ref, ke_ref, kd_ref, v_ref, gate_ref, y_ref, state_ref):
    n_chunks = HG_TS // HG_CHUNK
    seqs = range(HG_SEQS)

    @pl.when(pl.program_id(1) == 0)
    def _():
        state_ref[...] = jnp.zeros_like(state_ref)

    logits = lbl_ref[...]
    top = jnp.max(logits, axis=0, keepdims=True)
    e = jnp.exp(logits - top)
    lower = e[0:1, :] / jnp.sum(e, axis=0, keepdims=True)

    row = lax.broadcasted_iota(jnp.int32, (HG_TS, HG_TS), 0)
    col = lax.broadcasted_iota(jnp.int32, (HG_TS, HG_TS), 1)
    causal = (col <= row) & (col // HG_CHUNK == row // HG_CHUNK)
    tril = jnp.where(causal, 1.0, 0.0).astype(BF16)

    g = g_ref[...]
    hn = [_rms(h_ref[t], g).astype(BF16) for t in seqs]

    forget = [lower + (1.0 - lower) * _sigmoid(_dot(hn[t], win_ref[:, HG_FD:2 * HG_FD]))
              for t in seqs]
    k = [1.0 - forget[t] for t in seqs]
    log_f = [jnp.log(forget[t]) for t in seqs]
    hi = [log_f[t].astype(BF16) for t in seqs]
    lo = [(log_f[t] - hi[t].astype(F32)).astype(BF16) for t in seqs]
    b = [_dot(tril, hi[t]) + _dot(tril, lo[t]) for t in seqs]
    q = [_dot(hn[t], win_ref[:, 0:HG_FD]) for t in seqs]
    for t in seqs:
        qe_ref[t] = (q[t] * _sigmoid(q[t]) * jnp.exp(b[t])).astype(BF16)
        ke_ref[t] = (k[t] * jnp.exp(-b[t])).astype(BF16)
    for t in seqs:
        v_ref[t] = _dot(hn[t], win_ref[:, 2 * HG_FD:2 * HG_FD + D_MODEL]).astype(BF16)
    decay = [[] for _ in seqs]
    for t in seqs:
        for c in range(n_chunks):
            rs = slice(c * HG_CHUNK, (c + 1) * HG_CHUNK)
            b_end = b[t][(c + 1) * HG_CHUNK - 1:(c + 1) * HG_CHUNK, :]
            kd_ref[t, rs, :] = (k[t][rs, :] * jnp.exp(b_end - b[t][rs, :])).astype(BF16)
            decay[t].append(jnp.exp(b_end))
    for t in seqs:
        gate = _dot(hn[t], win_ref[:, 2 * HG_FD + D_MODEL:])
        gate_ref[t] = gate * _sigmoid(gate)

    og = og_ref[...]
    heads = range(HG_HEADS)
    ks = [slice(hd * HG_DK, (hd + 1) * HG_DK) for hd in heads]
    vs = [slice(hd * HG_DV, (hd + 1) * HG_DV) for hd in heads]
    pad = jnp.zeros((HG_CHUNK, HG_DK), BF16)
    o_intra = [[None] * HG_HEADS for _ in seqs]
    incr = [[[None] * n_chunks for _ in heads] for _ in seqs]

    def within_chunks(t):
        scores = []
        for hd in heads:
            scores.append(jnp.where(causal, _dot_nt(qe_ref[t, :, ks[hd]], ke_ref[t, :, ks[hd]]),
                                    0.0).astype(BF16))
            yield
        for hd in heads:
            o_intra[t][hd] = _dot(scores[hd], v_ref[t, :, vs[hd]])
            yield
        for c in range(0, n_chunks, 2):
            r0 = slice(c * HG_CHUNK, (c + 1) * HG_CHUNK)
            r1 = slice((c + 1) * HG_CHUNK, (c + 2) * HG_CHUNK)
            both = slice(c * HG_CHUNK, (c + 2) * HG_CHUNK)
            for hd in heads:
                kd_pair = jnp.concatenate(
                    [jnp.concatenate([kd_ref[t, r0, ks[hd]], pad], axis=1),
                     jnp.concatenate([pad, kd_ref[t, r1, ks[hd]]], axis=1)], axis=0)
                pair = _dot_tn(v_ref[t, both, vs[hd]], kd_pair)
                incr[t][hd][c] = pair[:, :HG_DK]
                incr[t][hd][c + 1] = pair[:, HG_DK:]
                yield

    def across_chunks(t):
        st = [state_ref[t, hd] for hd in heads]
        o_parts = [[] for _ in heads]
        for c in range(n_chunks):
            rs = slice(c * HG_CHUNK, (c + 1) * HG_CHUNK)
            for hd in heads:
                o_parts[hd].append(o_intra[t][hd][rs, :]
                                   + _dot_nt(qe_ref[t, rs, ks[hd]], st[hd].astype(BF16)))
                yield
            for hd in heads:
                st[hd] = decay[t][c][:, ks[hd]] * st[hd] + incr[t][hd][c]
        for hd in heads:
            state_ref[t, hd] = st[hd]
            o = jnp.concatenate(o_parts[hd], axis=0)
            y_ref[t, :, vs[hd]] = (_rms(o, og) * gate_ref[t, :, vs[hd]]).astype(BF16)
            yield

    def project_out(t):
        o_ref[t] = h_ref[t] + _dot(y_ref[t], wout_ref[...])
        yield

    def emit_together(*stages):
        live = list(stages)
        while live:
            for stage in list(live):
                if next(stage, _DONE) is _DONE:
                    live.remove(stage)

    emit_together(within_chunks(0))
    for t in range(1, HG_SEQS + 2):
        emit_together(*[stage(t - lag) for lag, stage in
                        enumerate((within_chunks, across_chunks, project_out))
                        if 0 <= t - lag < HG_SEQS])


def _hgrn(h, g, w_in, lb_logits, onorm_g, w_out):
    batch, seq, _ = h.shape
    steps = seq // HG_TS
    n_lb = lb_logits.shape[0]
    tile = pl.BlockSpec((HG_SEQS, HG_TS, D_MODEL), lambda b, s: (b, s, 0))
    return pl.pallas_call(
        _hgrn_kernel,
        out_shape=jax.ShapeDtypeStruct(h.shape, F32),
        grid=(batch // HG_SEQS, steps),
        in_specs=[
            tile,
            _resident((1, D_MODEL)),
            _resident((D_MODEL, 2 * HG_FD + 2 * D_MODEL)),
            _resident((n_lb, HG_FD)),
            _resident((1, HG_DV)),
            _resident((D_MODEL, D_MODEL)),
        ],
        out_specs=tile,
        scratch_shapes=[
            pltpu.VMEM((HG_SEQS, HG_TS, HG_FD), BF16),
            pltpu.VMEM((HG_SEQS, HG_TS, HG_FD), BF16),
            pltpu.VMEM((HG_SEQS, HG_TS, HG_FD), BF16),
            pltpu.VMEM((HG_SEQS, HG_TS, D_MODEL), BF16),
            pltpu.VMEM((HG_SEQS, HG_TS, D_MODEL), F32),
            pltpu.VMEM((HG_SEQS, HG_TS, D_MODEL), BF16),
            pltpu.VMEM((HG_SEQS, HG_HEADS, HG_DV, HG_DK), F32),
        ],
        compiler_params=pltpu.CompilerParams(
            dimension_semantics=("parallel", "arbitrary"), vmem_limit_bytes=VMEM_LIMIT),
        name="hgrn",
    )(h, g, w_in, lb_logits, onorm_g, w_out)


def _swa_kernel(sink_ref, h_ref, g_ref, wq_ref, qg_ref, cos_ref, sin_ref,
                halo_ref, kv_ref, wo_ref, o_ref, q_buf, kvar, vvar, o_buf):
    x = h_ref[...]
    hn = _rms(x, g_ref[...]).astype(BF16)
    q = _dot(hn, wq_ref[...])
    wide = 2 * LANES
    ones = _group_ones(wide, ATT_HEAD_DIM)
    src = lax.broadcasted_iota(jnp.int32, (wide, wide), 0)
    dst = lax.broadcasted_iota(jnp.int32, (wide, wide), 1)
    dim = dst % ATT_HEAD_DIM
    half_rot = ROT_DIM // 2
    partner = jnp.where((dim < half_rot) & (src == dst + half_rot), -1.0,
                        jnp.where((dim >= half_rot) & (dim < ROT_DIM) & (src == dst - half_rot),
                                  1.0, 0.0)).astype(BF16)
    qg = jnp.concatenate([qg_ref[...]] * 2, axis=1)
    cos_t = jnp.concatenate([cos_ref[...]] * 2, axis=1)
    sin_t = jnp.concatenate([sin_ref[...]] * 2, axis=1)
    for s2 in range(Q_SLABS // 2):
        qq = q[:, s2 * wide:(s2 + 1) * wide]
        ms = _dot((qq * qq).astype(BF16), ones) * (1.0 / ATT_HEAD_DIM)
        qn = qq * lax.rsqrt(ms + EPS) * qg
        qs = qn * cos_t + _dot(qn.astype(BF16), partner) * sin_t
        q_buf[:, s2 * wide:(s2 + 1) * wide] = (qs * (ATT_SCALE * LOG2E)).astype(BF16)

    kv_all = jnp.concatenate([halo_ref[...], kv_ref[...]], axis=0)
    k_plain, k_swap = kv_all[:, 0:LANES], kv_all[:, LANES:2 * LANES]
    v_plain, v_swap = kv_all[:, 2 * LANES:3 * LANES], kv_all[:, 3 * LANES:]
    low = lax.broadcasted_iota(jnp.int32, k_plain.shape, 1) < ATT_HEAD_DIM
    zero = jnp.zeros_like(k_plain)
    ones_lo = jnp.where(low, 1.0, 0.0).astype(BF16)
    ones_hi = jnp.where(low, 0.0, 1.0).astype(BF16)
    k_src = ((k_plain, True), (k_swap, False), (k_swap, True), (k_plain, False))
    v_src = ((v_plain, True), (v_swap, False), (v_swap, True), (v_plain, False))
    for i, ((kt, in_low), (vt, _)) in enumerate(zip(k_src, v_src)):
        kvar[:, i * LANES:(i + 1) * LANES] = jnp.where(low, kt, zero) if in_low else jnp.where(low, zero, kt)
        vv = jnp.where(low, vt, zero) if in_low else jnp.where(low, zero, vt)
        vvar[:, 2 * i * LANES:2 * (i + 1) * LANES] = jnp.concatenate(
            [vv, ones_lo if in_low else ones_hi], axis=1)

    cur_ok = (lax.broadcasted_iota(jnp.int32, (WINDOW, WINDOW), 1)
              <= lax.broadcasted_iota(jnp.int32, (WINDOW, WINDOW), 0))
    low_q = lax.broadcasted_iota(jnp.int32, (WINDOW, LANES), 1) < ATT_HEAD_DIM
    at_seq_start = pl.program_id(1) == 0

    def block(qb, carry):
        r0 = pl.multiple_of(qb * WINDOW, WINDOW)
        rows = pl.ds(r0, WINDOW)
        band = pl.ds(r0, 2 * WINDOW)
        prev_bias = jnp.where(jnp.logical_and(at_seq_start, qb == 0), MASKED, 0.0)
        q_st = [jnp.concatenate(
            [q_buf[rows, (kvh * SLABS_PER_KV + p) * LANES:(kvh * SLABS_PER_KV + p + 1) * LANES]
             for p in range(SLABS_PER_KV)], axis=0) for kvh in range(ATT_KV_HEADS)]
        acc = [None] * ATT_KV_HEADS
        sink_e = [[] for _ in range(ATT_KV_HEADS)]
        for par in range(HEADS_PER_SLAB):
            s_all = [_dot_nt(q_st[kvh], kvar[band, (kvh * HEADS_PER_SLAB + par) * LANES:
                                             (kvh * HEADS_PER_SLAB + par + 1) * LANES])
                     for kvh in range(ATT_KV_HEADS)]
            for kvh in range(ATT_KV_HEADS):
                slab = kvh * HEADS_PER_SLAB + par
                es = []
                se = []
                for p in range(SLABS_PER_KV):
                    head = kvh * ATT_GROUP + p * HEADS_PER_SLAB + par
                    sink = sink_ref[head] * LOG2E
                    pr = slice(p * WINDOW, (p + 1) * WINDOW)
                    z = jnp.where(cur_ok, s_all[kvh][pr, WINDOW:], s_all[kvh][pr, 0:WINDOW] + prev_bias)
                    m = jnp.maximum(jnp.max(z, axis=1, keepdims=True), sink)
                    ex = jnp.exp2(z - m)
                    es.append(jnp.concatenate(
                        [jnp.where(cur_ok, 0.0, ex), jnp.where(cur_ok, ex, 0.0)],
                        axis=1).astype(BF16))
                    se.append(jnp.exp2(sink - m))
                e_all = jnp.concatenate(es, axis=0)
                part = _dot(e_all, vvar[band, 2 * slab * LANES:2 * (slab + 1) * LANES])
                acc[kvh] = part if acc[kvh] is None else acc[kvh] + part
                sink_e[kvh].append(se)
        for kvh in range(ATT_KV_HEADS):
            for p in range(SLABS_PER_KV):
                pr = slice(p * WINDOW, (p + 1) * WINDOW)
                den = acc[kvh][pr, LANES:] + jnp.where(low_q, sink_e[kvh][0][p], sink_e[kvh][1][p])
                sl = (kvh * SLABS_PER_KV + p) * LANES
                o_buf[rows, sl:sl + LANES] = (acc[kvh][pr, :LANES] / den).astype(BF16)
        return carry

    lax.fori_loop(0, ATT_TQ // WINDOW, block, 0, unroll=True)
    o_ref[...] = x + _dot(o_buf[...], wo_ref[...])


def _swa(h, g, w_q, qg, sinks, cos_t, sin_t, kv, w_o, batch, seq):
    steps = seq // ATT_TQ
    blocks_per_step = ATT_TQ // WINDOW
    blocks_per_seq = seq // WINDOW

    def halo_map(b, s):
        return (jnp.maximum(b * blocks_per_seq + s * blocks_per_step - 1, 0), 0)

    return pl.pallas_call(
        _swa_kernel,
        out_shape=jax.ShapeDtypeStruct(h.shape, F32),
        grid=(batch, steps),
        in_specs=[
            pl.BlockSpec(memory_space=pltpu.SMEM),
            pl.BlockSpec((ATT_TQ, D_MODEL), lambda b, s: (b * steps + s, 0)),
            _resident((1, D_MODEL)),
            _resident((D_MODEL, D_MODEL)),
            _resident((1, LANES)),
            pl.BlockSpec((ATT_TQ, LANES), lambda b, s: (s, 0)),
            pl.BlockSpec((ATT_TQ, LANES), lambda b, s: (s, 0)),
            pl.BlockSpec((WINDOW, KV_LANES), halo_map),
            pl.BlockSpec((ATT_TQ, KV_LANES), lambda b, s: (b * steps + s, 0)),
            _resident((D_MODEL, D_MODEL)),
        ],
        out_specs=pl.BlockSpec((ATT_TQ, D_MODEL), lambda b, s: (b * steps + s, 0)),
        scratch_shapes=[
            pltpu.VMEM((ATT_TQ, D_MODEL), BF16),
            pltpu.VMEM((ATT_TQ + WINDOW, 4 * LANES), BF16),
            pltpu.VMEM((ATT_TQ + WINDOW, 8 * LANES), BF16),
            pltpu.VMEM((ATT_TQ, D_MODEL), BF16),
        ],
        compiler_params=pltpu.CompilerParams(
            dimension_semantics=("parallel", "parallel"), vmem_limit_bytes=VMEM_LIMIT),
        name="swa",
    )(sinks, h, g, w_q, qg, cos_t, sin_t, kv, kv, w_o)


def _rope_tables(seq):
    half = ROT_DIM // 2
    inv_freq = jnp.power(ROPE_THETA, -jnp.arange(half, dtype=F32) * (2.0 / ROT_DIM))
    ang = jnp.arange(seq).astype(F32)[:, None] * inv_freq[None, :]
    cos, sin = jnp.cos(ang), jnp.sin(ang)
    zeros = jnp.zeros_like(sin)
    rest = ATT_HEAD_DIM - ROT_DIM
    pad0 = jnp.zeros((seq, rest), F32)
    cos_t = jnp.concatenate([cos, cos, jnp.ones((seq, rest), F32)], axis=1)
    sin_lo = jnp.concatenate([-sin, zeros, pad0], axis=1)
    sin_hi = jnp.concatenate([zeros, sin, pad0], axis=1)
    sin_t = jnp.concatenate([sin, sin, pad0], axis=1)
    return tuple(jnp.tile(t, (1, HEADS_PER_SLAB)) for t in (cos_t, sin_lo, sin_hi, sin_t))


def kernel(x, ffn_norm_g, ffn_w_gate_up, ffn_w_down, mix_norm_g, hgrn_w_in, hgrn_lb_logits,
           hgrn_onorm_g, hgrn_w_out, kv_norm_g, kv_w, k_norm_g, attn_w_q, q_norm_g,
           attn_sinks, attn_w_out):
    batch, seq, d = x.shape
    assert d == D_MODEL and seq % ATT_TQ == 0 and seq % HG_TS == 0 and batch % HG_SEQS == 0
    assert (batch * seq) % FFN_TM == 0 and seq % FFN_TM == 0
    h = x.reshape(batch * seq, d)

    cos_t, sin_lo, sin_hi, sin_t = _rope_tables(seq)
    ffn_g = ffn_norm_g.reshape(ffn_norm_g.shape[0], 2, 1, D_MODEL)

    h, w_gu01, w_d01, w_in, w_hout = _ffn(
        h, ffn_g[0, 0], ffn_w_gate_up[0, 0].astype(BF16), ffn_w_down[0, 0].astype(BF16), "ffn_0_0",
        casts=((ffn_w_gate_up, (0, 1)), (ffn_w_down, (0, 1)), (hgrn_w_in, (0,)), (hgrn_w_out, (0,))))
    h = _hgrn(h.reshape(batch, seq, d), mix_norm_g[0].reshape(1, D_MODEL), w_in,
              hgrn_lb_logits, hgrn_onorm_g[0].reshape(1, HG_DV), w_hout).reshape(batch * seq, d)
    h, w_gu10, w_d10, w_kv, w_q, w_aout = _ffn(
        h, ffn_g[0, 1], w_gu01, w_d01, "ffn_0_1",
        casts=((ffn_w_gate_up, (1, 0)), (ffn_w_down, (1, 0)), (kv_w, ()), (attn_w_q, (0,)),
               (attn_w_out, (0,))))
    h, kv, w_gu11, w_d11 = _ffn(
        h, ffn_g[1, 0], w_gu10, w_d10, "ffn_1_0",
        kv_args=(kv_norm_g.reshape(1, D_MODEL), w_kv,
                 jnp.tile(k_norm_g, HEADS_PER_SLAB).reshape(1, LANES),
                 cos_t, sin_lo, sin_hi, seq),
        casts=((ffn_w_gate_up, (1, 1)), (ffn_w_down, (1, 1))))
    h = _swa(h, mix_norm_g[1].reshape(1, D_MODEL), w_q,
             jnp.tile(q_norm_g[0], HEADS_PER_SLAB).reshape(1, LANES), attn_sinks[0],
             cos_t, sin_t, kv, w_aout, batch, seq)
    (h,) = _ffn(h, ffn_g[1, 1], w_gu11, w_d11, "ffn_1_1")
    return h.reshape(batch, seq, d)
```

```python
import functools

import jax
import jax.numpy as jnp
from jax import lax
from jax.experimental import pallas as pl
from jax.experimental.pallas import tpu as pltpu

D_MODEL = 1024
D_FF = 2816
MACARON_WEIGHT = 0.5

HG_HEADS = 8
HG_DK = 128
HG_FD = HG_HEADS * HG_DK
HG_DV = D_MODEL // HG_HEADS
HG_CHUNK = 64

ATT_HEADS = 16
ATT_KV_HEADS = 2
ATT_GROUP = ATT_HEADS // ATT_KV_HEADS
ATT_HEAD_DIM = 64
WINDOW = 128
ATT_SCALE = ATT_HEAD_DIM ** -0.5
ROPE_THETA = 500000.0
ROT_DIM = ATT_HEAD_DIM // 4
EPS = 1e-6

LANES = 128
BF16_SUBLANES = 16
HEADS_PER_SLAB = LANES // ATT_HEAD_DIM
Q_SLABS = ATT_HEADS // HEADS_PER_SLAB
SLABS_PER_KV = Q_SLABS // ATT_KV_HEADS
MASKED = -1e30
LOG2E = 1.4426950408889634
KV_LANES = 4 * LANES

FFN_TM = 1024
FFN_SPLIT = 2
FFN_CF = 256
HG_TS = 256
HG_SEQS = 4
ATT_TQ = 1024
VMEM_LIMIT = 56 * 1024 * 1024

BF16 = jnp.bfloat16
F32 = jnp.float32
_DONE = object()


def _dot(a, b):
    return jnp.dot(a, b, preferred_element_type=F32)


def _dot_nt(a, b):
    return lax.dot_general(a, b, (((1,), (1,)), ((), ())), preferred_element_type=F32)


def _dot_tn(a, b):
    return lax.dot_general(a, b, (((0,), (0,)), ((), ())), preferred_element_type=F32)


def _rms(x, g):
    ms = jnp.mean(x * x, axis=-1, keepdims=True)
    return x * lax.rsqrt(ms + EPS) * g


def _sigmoid(x):
    return 1.0 / (1.0 + jnp.exp(-x))


def _group_mean_sq(x, group_ones, inv_n):
    x2 = x * x
    hi = x2.astype(BF16)
    lo = (x2 - hi.astype(F32)).astype(BF16)
    return (_dot(hi, group_ones) + _dot(lo, group_ones)) * inv_n


def _group_ones(n, group):
    r = lax.broadcasted_iota(jnp.int32, (n, n), 0) // group
    c = lax.broadcasted_iota(jnp.int32, (n, n), 1) // group
    return jnp.where(r == c, 1.0, 0.0).astype(BF16)


def _rope_slab(x, cos_t, sin_lo, sin_hi):
    return (x * cos_t + pltpu.roll(x, LANES - ROT_DIM // 2, axis=1) * sin_lo
            + pltpu.roll(x, ROT_DIM // 2, axis=1) * sin_hi)


def _resident(shape):
    nd = len(shape)
    return pl.BlockSpec(shape, lambda *_: (0,) * nd, pipeline_mode=pl.Buffered(1))


def _inv_rms(x):
    return lax.rsqrt(jnp.mean(x * x, axis=-1, keepdims=True) + EPS)


def _swiglu_residual(x_ref, g, wgu_ref, wd_ref, o_ref, before=None):
    rows = FFN_TM // FFN_SPLIT
    groups = [slice(i * rows, (i + 1) * rows) for i in range(FFN_SPLIT)]
    inv = [_inv_rms(x_ref[r, :]) for r in groups]
    hns = [(x_ref[r, :] * inv[i] * g).astype(BF16) for i, r in enumerate(groups)]
    if before is not None:
        before(inv)
    accs = [None] * FFN_SPLIT
    for c in range(D_FF // FFN_CF):
        lo = c * FFN_CF
        for i, hn in enumerate(hns):
            gate = _dot(hn, wgu_ref[:, lo:lo + FFN_CF])
            up = _dot(hn, wgu_ref[:, D_FF + lo:D_FF + lo + FFN_CF])
            act = (gate * _sigmoid(gate) * up).astype(BF16)
            part = _dot(act, wd_ref[lo:lo + FFN_CF, :])
            accs[i] = part if accs[i] is None else accs[i] + part
    for r, acc in zip(groups, accs):
        o_ref[r, :] = x_ref[r, :] + MACARON_WEIGHT * acc


def _shared_kv(i, inv, x_ref, kvg_ref, kvw_ref, kg_ref, cos_ref, slo_ref, shi_ref, kv_ref):
    rows = FFN_TM // FFN_SPLIT
    r = slice(i * rows, (i + 1) * rows)
    hn = (x_ref[r, :] * inv * kvg_ref[...]).astype(BF16)
    kv = _dot(hn, kvw_ref[...])
    k = kv[:, :LANES]
    v = kv[:, LANES:]
    ms = _group_mean_sq(k, _group_ones(LANES, ATT_HEAD_DIM), 1.0 / ATT_HEAD_DIM)
    k = k * lax.rsqrt(ms + EPS) * kg_ref[...]
    k = _rope_slab(k, cos_ref[r, :], slo_ref[r, :], shi_ref[r, :])
    kv_ref[r, 0 * LANES:1 * LANES] = k.astype(BF16)
    kv_ref[r, 1 * LANES:2 * LANES] = pltpu.roll(k, ATT_HEAD_DIM, axis=1).astype(BF16)
    kv_ref[r, 2 * LANES:3 * LANES] = v.astype(BF16)
    kv_ref[r, 3 * LANES:4 * LANES] = pltpu.roll(v, ATT_HEAD_DIM, axis=1).astype(BF16)


def _ffn_kernel(*refs, with_kv, n_cast):
    n_in = 4 + (6 if with_kv else 0) + n_cast
    ins, outs = refs[:n_in], refs[n_in:]
    x_ref, g_ref, wgu_ref, wd_ref = ins[:4]
    for src_ref, dst_ref in zip(ins[n_in - n_cast:], outs[len(outs) - n_cast:]):
        dst_ref[...] = src_ref[...].astype(BF16)

    def kv_first(inv):
        for i in range(FFN_SPLIT):
            _shared_kv(i, inv[i], x_ref, *ins[4:10], outs[1])

    _swiglu_residual(x_ref, g_ref[...], wgu_ref, wd_ref, outs[0],
                     before=kv_first if with_kv else None)


def _cast_rows(n_rows, steps):
    rows = BF16_SUBLANES
    while n_rows % rows or n_rows // rows > steps:
        rows += BF16_SUBLANES
    return rows


def _ffn(h, g, w_gu, w_d, name, kv_args=None, casts=()):
    t = h.shape[0]
    steps = t // FFN_TM
    in_specs = [
        pl.BlockSpec((FFN_TM, D_MODEL), lambda i: (i, 0)),
        _resident((1, D_MODEL)),
        _resident((D_MODEL, 2 * D_FF)),
        _resident((D_FF, D_MODEL)),
    ]
    out_shape = [jax.ShapeDtypeStruct((t, D_MODEL), F32)]
    out_specs = [pl.BlockSpec((FFN_TM, D_MODEL), lambda i: (i, 0))]
    args = [h, g, w_gu, w_d]
    if kv_args is not None:
        seq = kv_args[-1]
        per_seq = seq // FFN_TM
        table = pl.BlockSpec((FFN_TM, LANES), lambda i: (i % per_seq, 0))
        in_specs += [_resident((1, D_MODEL)), _resident((D_MODEL, 2 * LANES)),
                     _resident((1, LANES)), table, table, table]
        out_shape.append(jax.ShapeDtypeStruct((t, KV_LANES), BF16))
        out_specs.append(pl.BlockSpec((FFN_TM, KV_LANES), lambda i: (i, 0)))
        args += list(kv_args[:-1])
    for w, lead in casts:
        n_rows, cols = w.shape[len(lead):]
        rows = _cast_rows(n_rows, steps)
        last = n_rows // rows - 1
        in_specs.append(pl.BlockSpec(
            (None,) * len(lead) + (rows, cols),
            lambda i, lead=lead, last=last: lead + (jnp.minimum(i, last), 0)))
        out_specs.append(pl.BlockSpec((rows, cols), lambda i, last=last: (jnp.minimum(i, last), 0)))
        out_shape.append(jax.ShapeDtypeStruct((n_rows, cols), BF16))
        args.append(w)
    return pl.pallas_call(
        functools.partial(_ffn_kernel, with_kv=kv_args is not None, n_cast=len(casts)),
        out_shape=tuple(out_shape),
        grid=(steps,),
        in_specs=in_specs,
        out_specs=tuple(out_specs),
        compiler_params=pltpu.CompilerParams(
            dimension_semantics=("arbitrary",), vmem_limit_bytes=VMEM_LIMIT),
        name=name,
    )(*args)


def _hgrn_kernel(h_ref, g_ref, win_ref, lbl_ref, og_ref, wout_ref, o_ref,
                 qe_ref, ke_ref, kd_ref, v_ref, gate_ref, y_ref, state_ref):
    n_chunks = HG_TS // HG_CHUNK
    seqs = range(HG_SEQS)

    @pl.when(pl.program_id(1) == 0)
    def _():
        state_ref[...] = jnp.zeros_like(state_ref)

    logits = lbl_ref[...]
    top = jnp.max(logits, axis=0, keepdims=True)
    e = jnp.exp(logits - top)
    lower = e[0:1, :] / jnp.sum(e, axis=0, keepdims=True)

    row = lax.broadcasted_iota(jnp.int32, (HG_TS, HG_TS), 0)
    col = lax.broadcasted_iota(jnp.int32, (HG_TS, HG_TS), 1)
    causal = (col <= row) & (col // HG_CHUNK == row // HG_CHUNK)
    tril = jnp.where(causal, 1.0, 0.0).astype(BF16)

    g = g_ref[...]
    hn = [_rms(h_ref[t], g).astype(BF16) for t in seqs]

    forget = [lower + (1.0 - lower) * _sigmoid(_dot(hn[t], win_ref[:, HG_FD:2 * HG_FD]))
              for t in seqs]
    k = [1.0 - forget[t] for t in seqs]
    log_f = [jnp.log(forget[t]) for t in seqs]
    hi = [log_f[t].astype(BF16) for t in seqs]
    lo = [(log_f[t] - hi[t].astype(F32)).astype(BF16) for t in seqs]
    b = [_dot(tril, hi[t]) + _dot(tril, lo[t]) for t in seqs]
    q = [_dot(hn[t], win_ref[:, 0:HG_FD]) for t in seqs]
    for t in seqs:
        qe_ref[t] = (q[t] * _sigmoid(q[t]) * jnp.exp(b[t])).astype(BF16)
        ke_ref[t] = (k[t] * jnp.exp(-b[t])).astype(BF16)
    for t in seqs:
        v_ref[t] = _dot(hn[t], win_ref[:, 2 * HG_FD:2 * HG_FD + D_MODEL]).astype(BF16)
    decay = [[] for _ in seqs]
    for t in seqs:
        for c in range(n_chunks):
            rs = slice(c * HG_CHUNK, (c + 1) * HG_CHUNK)
            b_end = b[t][(c + 1) * HG_CHUNK - 1:(c + 1) * HG_CHUNK, :]
            kd_ref[t, rs, :] = (k[t][rs, :] * jnp.exp(b_end - b[t][rs, :])).astype(BF16)
            decay[t].append(jnp.exp(b_end))
    for t in seqs:
        gate = _dot(hn[t], win_ref[:, 2 * HG_FD + D_MODEL:])
        gate_ref[t] = gate * _sigmoid(gate)

    og = og_ref[...]
    heads = range(HG_HEADS)
    ks = [slice(hd * HG_DK, (hd + 1) * HG_DK) for hd in heads]
    vs = [slice(hd * HG_DV, (hd + 1) * HG_DV) for hd in heads]
    pad = jnp.zeros((HG_CHUNK, HG_DK), BF16)
    o_intra = [[None] * HG_HEADS for _ in seqs]
    incr = [[[None] * n_chunks for _ in heads] for _ in seqs]

    def within_chunks(t):
        scores = []
        for hd in heads:
            scores.append(jnp.where(causal, _dot_nt(qe_ref[t, :, ks[hd]], ke_ref[t, :, ks[hd]]),
                                    0.0).astype(BF16))
            yield
        for hd in heads:
            o_intra[t][hd] = _dot(scores[hd], v_ref[t, :, vs[hd]])
            yield
        for c in range(0, n_chunks, 2):
            r0 = slice(c * HG_CHUNK, (c + 1) * HG_CHUNK)
            r1 = slice((c + 1) * HG_CHUNK, (c + 2) * HG_CHUNK)
            both = slice(c * HG_CHUNK, (c + 2) * HG_CHUNK)
            for hd in heads:
                kd_pair = jnp.concatenate(
                    [jnp.concatenate([kd_ref[t, r0, ks[hd]], pad], axis=1),
                     jnp.concatenate([pad, kd_ref[t, r1, ks[hd]]], axis=1)], axis=0)
                pair = _dot_tn(v_ref[t, both, vs[hd]], kd_pair)
                incr[t][hd][c] = pair[:, :HG_DK]
                incr[t][hd][c + 1] = pair[:, HG_DK:]
                yield

    def across_chunks(t):
        st = [state_ref[t, hd] for hd in heads]
        o_parts = [[] for _ in heads]
        for c in range(n_chunks):
            rs = slice(c * HG_CHUNK, (c + 1) * HG_CHUNK)
            for hd in heads:
                o_parts[hd].append(o_intra[t][hd][rs, :]
                                   + _dot(qe_ref[t, rs, ks[hd]], st[hd].T.astype(BF16)))
                yield
            for hd in heads:
                st[hd] = decay[t][c][:, ks[hd]] * st[hd] + incr[t][hd][c]
        for hd in heads:
            state_ref[t, hd] = st[hd]
            o = jnp.concatenate(o_parts[hd], axis=0)
            y_ref[t, :, vs[hd]] = (_rms(o, og) * gate_ref[t, :, vs[hd]]).astype(BF16)
            yield

    def project_out(t):
        o_ref[t] = h_ref[t] + _dot(y_ref[t], wout_ref[...])
        yield

    def emit_together(*stages):
        live = list(stages)
        while live:
            for stage in list(live):
                if next(stage, _DONE) is _DONE:
                    live.remove(stage)

    emit_together(within_chunks(0))
    for t in range(1, HG_SEQS + 2):
        emit_together(*[stage(t - lag) for lag, stage in
                        enumerate((within_chunks, across_chunks, project_out))
                        if 0 <= t - lag < HG_SEQS])


def _hgrn(h, g, w_in, lb_logits, onorm_g, w_out):
    batch, seq, _ = h.shape
    steps = seq // HG_TS
    n_lb = lb_logits.shape[0]
    tile = pl.BlockSpec((HG_SEQS, HG_TS, D_MODEL), lambda b, s: (b, s, 0))
    return pl.pallas_call(
        _hgrn_kernel,
        out_shape=jax.ShapeDtypeStruct(h.shape, F32),
        grid=(batch // HG_SEQS, steps),
        in_specs=[
            tile,
            _resident((1, D_MODEL)),
            _resident((D_MODEL, 2 * HG_FD + 2 * D_MODEL)),
            _resident((n_lb, HG_FD)),
            _resident((1, HG_DV)),
            _resident((D_MODEL, D_MODEL)),
        ],
        out_specs=tile,
        scratch_shapes=[
            pltpu.VMEM((HG_SEQS, HG_TS, HG_FD), BF16),
            pltpu.VMEM((HG_SEQS, HG_TS, HG_FD), BF16),
            pltpu.VMEM((HG_SEQS, HG_TS, HG_FD), BF16),
            pltpu.VMEM((HG_SEQS, HG_TS, D_MODEL), BF16),
            pltpu.VMEM((HG_SEQS, HG_TS, D_MODEL), F32),
            pltpu.VMEM((HG_SEQS, HG_TS, D_MODEL), BF16),
            pltpu.VMEM((HG_SEQS, HG_HEADS, HG_DV, HG_DK), F32),
        ],
        compiler_params=pltpu.CompilerParams(
            dimension_semantics=("parallel", "arbitrary"), vmem_limit_bytes=VMEM_LIMIT),
        name="hgrn",
    )(h, g, w_in, lb_logits, onorm_g, w_out)


def _swa_kernel(sink_ref, h_ref, g_ref, wq_ref, qg_ref, cos_ref, sin_ref,
                halo_ref, kv_ref, wo_ref, o_ref, q_buf, kvar, vvar, o_buf):
    x = h_ref[...]
    hn = _rms(x, g_ref[...]).astype(BF16)
    q = _dot(hn, wq_ref[...])
    wide = 2 * LANES
    ones = _group_ones(wide, ATT_HEAD_DIM)
    src = lax.broadcasted_iota(jnp.int32, (wide, wide), 0)
    dst = lax.broadcasted_iota(jnp.int32, (wide, wide), 1)
    dim = dst % ATT_HEAD_DIM
    half_rot = ROT_DIM // 2
    partner = jnp.where((dim < half_rot) & (src == dst + half_rot), -1.0,
                        jnp.where((dim >= half_rot) & (dim < ROT_DIM) & (src == dst - half_rot),
                                  1.0, 0.0)).astype(BF16)
    qg = jnp.concatenate([qg_ref[...]] * 2, axis=1)
    cos_t = jnp.concatenate([cos_ref[...]] * 2, axis=1)
    sin_t = jnp.concatenate([sin_ref[...]] * 2, axis=1)
    for s2 in range(Q_SLABS // 2):
        qq = q[:, s2 * wide:(s2 + 1) * wide]
        ms = _dot((qq * qq).astype(BF16), ones) * (1.0 / ATT_HEAD_DIM)
        qn = qq * lax.rsqrt(ms + EPS) * qg
        qs = qn * cos_t + _dot(qn.astype(BF16), partner) * sin_t
        q_buf[:, s2 * wide:(s2 + 1) * wide] = (qs * (ATT_SCALE * LOG2E)).astype(BF16)

    kv_all = jnp.concatenate([halo_ref[...], kv_ref[...]], axis=0)
    k_plain, k_swap = kv_all[:, 0:LANES], kv_all[:, LANES:2 * LANES]
    v_plain, v_swap = kv_all[:, 2 * LANES:3 * LANES], kv_all[:, 3 * LANES:]
    low = lax.broadcasted_iota(jnp.int32, k_plain.shape, 1) < ATT_HEAD_DIM
    zero = jnp.zeros_like(k_plain)
    ones_lo = jnp.where(low, 1.0, 0.0).astype(BF16)
    ones_hi = jnp.where(low, 0.0, 1.0).astype(BF16)
    k_src = ((k_plain, True), (k_swap, False), (k_swap, True), (k_plain, False))
    v_src = ((v_plain, True), (v_swap, False), (v_swap, True), (v_plain, False))
    for i, ((kt, in_low), (vt, _)) in enumerate(zip(k_src, v_src)):
        kvar[:, i * LANES:(i + 1) * LANES] = jnp.where(low, kt, zero) if in_low else jnp.where(low, zero, kt)
        vv = jnp.where(low, vt, zero) if in_low else jnp.where(low, zero, vt)
        vvar[:, 2 * i * LANES:2 * (i + 1) * LANES] = jnp.concatenate(
            [vv, ones_lo if in_low else ones_hi], axis=1)

    cur_ok = (lax.broadcasted_iota(jnp.int32, (WINDOW, WINDOW), 1)
              <= lax.broadcasted_iota(jnp.int32, (WINDOW, WINDOW), 0))
    low_q = lax.broadcasted_iota(jnp.int32, (WINDOW, LANES), 1) < ATT_HEAD_DIM
    at_seq_start = pl.program_id(1) == 0

    def block(qb, carry):
        r0 = pl.multiple_of(qb * WINDOW, WINDOW)
        rows = pl.ds(r0, WINDOW)
        band = pl.ds(r0, 2 * WINDOW)
        prev_bias = jnp.where(jnp.logical_and(at_seq_start, qb == 0), MASKED, 0.0)
        q_st = [jnp.concatenate(
            [q_buf[rows, (kvh * SLABS_PER_KV + p) * LANES:(kvh * SLABS_PER_KV + p + 1) * LANES]
             for p in range(SLABS_PER_KV)], axis=0) for kvh in range(ATT_KV_HEADS)]
        acc = [None] * ATT_KV_HEADS
        sink_e = [[] for _ in range(ATT_KV_HEADS)]
        for par in range(HEADS_PER_SLAB):
            s_all = [_dot_nt(q_st[kvh], kvar[band, (kvh * HEADS_PER_SLAB + par) * LANES:
                                             (kvh * HEADS_PER_SLAB + par + 1) * LANES])
                     for kvh in range(ATT_KV_HEADS)]
            for kvh in range(ATT_KV_HEADS):
                slab = kvh * HEADS_PER_SLAB + par
                es = []
                se = []
                for p in range(SLABS_PER_KV):
                    head = kvh * ATT_GROUP + p * HEADS_PER_SLAB + par
                    sink = sink_ref[head] * LOG2E
                    pr = slice(p * WINDOW, (p + 1) * WINDOW)
                    z = jnp.where(cur_ok, s_all[kvh][pr, WINDOW:], s_all[kvh][pr, 0:WINDOW] + prev_bias)
                    m = jnp.maximum(jnp.max(z, axis=1, keepdims=True), sink)
                    ex = jnp.exp2(z - m)
                    es.append(jnp.concatenate(
                        [jnp.where(cur_ok, 0.0, ex), jnp.where(cur_ok, ex, 0.0)],
                        axis=1).astype(BF16))
                    se.append(jnp.exp2(sink - m))
                e_all = jnp.concatenate(es, axis=0)
                part = _dot(e_all, vvar[band, 2 * slab * LANES:2 * (slab + 1) * LANES])
                acc[kvh] = part if acc[kvh] is None else acc[kvh] + part
                sink_e[kvh].append(se)
        for kvh in range(ATT_KV_HEADS):
            for p in range(SLABS_PER_KV):
                pr = slice(p * WINDOW, (p + 1) * WINDOW)
                den = acc[kvh][pr, LANES:] + jnp.where(low_q, sink_e[kvh][0][p], sink_e[kvh][1][p])
                sl = (kvh * SLABS_PER_KV + p) * LANES
                o_buf[rows, sl:sl + LANES] = (acc[kvh][pr, :LANES] / den).astype(BF16)
        return carry

    lax.fori_loop(0, ATT_TQ // WINDOW, block, 0, unroll=True)
    o_ref[...] = x + _dot(o_buf[...], wo_ref[...])


def _swa(h, g, w_q, qg, sinks, cos_t, sin_t, kv, w_o, batch, seq):
    steps = seq // ATT_TQ
    blocks_per_step = ATT_TQ // WINDOW
    blocks_per_seq = seq // WINDOW

    def halo_map(b, s):
        return (jnp.maximum(b * blocks_per_seq + s * blocks_per_step - 1, 0), 0)

    return pl.pallas_call(
        _swa_kernel,
        out_shape=jax.ShapeDtypeStruct(h.shape, F32),
        grid=(batch, steps),
        in_specs=[
            pl.BlockSpec(memory_space=pltpu.SMEM),
            pl.BlockSpec((ATT_TQ, D_MODEL), lambda b, s: (b * steps + s, 0)),
            _resident((1, D_MODEL)),
            _resident((D_MODEL, D_MODEL)),
            _resident((1, LANES)),
            pl.BlockSpec((ATT_TQ, LANES), lambda b, s: (s, 0)),
            pl.BlockSpec((ATT_TQ, LANES), lambda b, s: (s, 0)),
            pl.BlockSpec((WINDOW, KV_LANES), halo_map),
            pl.BlockSpec((ATT_TQ, KV_LANES), lambda b, s: (b * steps + s, 0)),
            _resident((D_MODEL, D_MODEL)),
        ],
        out_specs=pl.BlockSpec((ATT_TQ, D_MODEL), lambda b, s: (b * steps + s, 0)),
        scratch_shapes=[
            pltpu.VMEM((ATT_TQ, D_MODEL), BF16),
            pltpu.VMEM((ATT_TQ + WINDOW, 4 * LANES), BF16),
            pltpu.VMEM((ATT_TQ + WINDOW, 8 * LANES), BF16),
            pltpu.VMEM((ATT_TQ, D_MODEL), BF16),
        ],
        compiler_params=pltpu.CompilerParams(
            dimension_semantics=("parallel", "parallel"), vmem_limit_bytes=VMEM_LIMIT),
        name="swa",
    )(sinks, h, g, w_q, qg, cos_t, sin_t, kv, kv, w_o)


def _rope_tables(seq):
    half = ROT_DIM // 2
    inv_freq = jnp.power(ROPE_THETA, -jnp.arange(half, dtype=F32) * (2.0 / ROT_DIM))
    ang = jnp.arange(seq).astype(F32)[:, None] * inv_freq[None, :]
    cos, sin = jnp.cos(ang), jnp.sin(ang)
    zeros = jnp.zeros_like(sin)
    rest = ATT_HEAD_DIM - ROT_DIM
    pad0 = jnp.zeros((seq, rest), F32)
    cos_t = jnp.concatenate([cos, cos, jnp.ones((seq, rest), F32)], axis=1)
    sin_lo = jnp.concatenate([-sin, zeros, pad0], axis=1)
    sin_hi = jnp.concatenate([zeros, sin, pad0], axis=1)
    sin_t = jnp.concatenate([sin, sin, pad0], axis=1)
    return tuple(jnp.tile(t, (1, HEADS_PER_SLAB)) for t in (cos_t, sin_lo, sin_hi, sin_t))


def kernel(x, ffn_norm_g, ffn_w_gate_up, ffn_w_down, mix_norm_g, hgrn_w_in, hgrn_lb_logits,
           hgrn_onorm_g, hgrn_w_out, kv_norm_g, kv_w, k_norm_g, attn_w_q, q_norm_g,
           attn_sinks, attn_w_out):
    batch, seq, d = x.shape
    assert d == D_MODEL and seq % ATT_TQ == 0 and seq % HG_TS == 0 and batch % HG_SEQS == 0
    assert (batch * seq) % FFN_TM == 0 and seq % FFN_TM == 0
    h = x.reshape(batch * seq, d)

    cos_t, sin_lo, sin_hi, sin_t = _rope_tables(seq)
    ffn_g = ffn_norm_g.reshape(ffn_norm_g.shape[0], 2, 1, D_MODEL)

    h, w_gu01, w_d01, w_in, w_hout = _ffn(
        h, ffn_g[0, 0], ffn_w_gate_up[0, 0].astype(BF16), ffn_w_down[0, 0].astype(BF16), "ffn_0_0",
        casts=((ffn_w_gate_up, (0, 1)), (ffn_w_down, (0, 1)), (hgrn_w_in, (0,)), (hgrn_w_out, (0,))))
    h = _hgrn(h.reshape(batch, seq, d), mix_norm_g[0].reshape(1, D_MODEL), w_in,
              hgrn_lb_logits, hgrn_onorm_g[0].reshape(1, HG_DV), w_hout).reshape(batch * seq, d)
    h, w_gu10, w_d10, w_kv, w_q, w_aout = _ffn(
        h, ffn_g[0, 1], w_gu01, w_d01, "ffn_0_1",
        casts=((ffn_w_gate_up, (1, 0)), (ffn_w_down, (1, 0)), (kv_w, ()), (attn_w_q, (0,)),
               (attn_w_out, (0,))))
    h, kv, w_gu11, w_d11 = _ffn(
        h, ffn_g[1, 0], w_gu10, w_d10, "ffn_1_0",
        kv_args=(kv_norm_g.reshape(1, D_MODEL), w_kv,
                 jnp.tile(k_norm_g, HEADS_PER_SLAB).reshape(1, LANES),
                 cos_t, sin_lo, sin_hi, seq),
        casts=((ffn_w_gate_up, (1, 1)), (ffn_w_down, (1, 1))))
    h = _swa(h, mix_norm_g[1].reshape(1, D_MODEL), w_q,
             jnp.tile(q_norm_g[0], HEADS_PER_SLAB).reshape(1, LANES), attn_sinks[0],
             cos_t, sin_t, kv, w_aout, batch, seq)
    (h,) = _ffn(h, ffn_g[1, 1], w_gu11, w_d11, "ffn_1_1")
    return h.reshape(batch, seq, d)
```

```python
import functools

import jax
import jax.numpy as jnp
from jax import lax
from jax.experimental import pallas as pl
from jax.experimental.pallas import tpu as pltpu

D_MODEL = 1024
D_FF = 2816
MACARON_WEIGHT = 0.5

HG_HEADS = 8
HG_DK = 128
HG_FD = HG_HEADS * HG_DK
HG_DV = D_MODEL // HG_HEADS
HG_CHUNK = 64

ATT_HEADS = 16
ATT_KV_HEADS = 2
ATT_GROUP = ATT_HEADS // ATT_KV_HEADS
ATT_HEAD_DIM = 64
WINDOW = 128
ATT_SCALE = ATT_HEAD_DIM ** -0.5
ROPE_THETA = 500000.0
ROT_DIM = ATT_HEAD_DIM // 4
EPS = 1e-6

LANES = 128
BF16_SUBLANES = 16
HEADS_PER_SLAB = LANES // ATT_HEAD_DIM
Q_SLABS = ATT_HEADS // HEADS_PER_SLAB
SLABS_PER_KV = Q_SLABS // ATT_KV_HEADS
MASKED = -1e30
LOG2E = 1.4426950408889634
KV_LANES = 4 * LANES

FFN_TM = 1024
FFN_SPLIT = 2
FFN_CF = 256
HG_TS = 256
HG_SEQS = 4
ATT_TQ = 1024
VMEM_LIMIT = 56 * 1024 * 1024

BF16 = jnp.bfloat16
F32 = jnp.float32
_DONE = object()


def _dot(a, b):
    return jnp.dot(a, b, preferred_element_type=F32)


def _dot_nt(a, b):
    return lax.dot_general(a, b, (((1,), (1,)), ((), ())), preferred_element_type=F32)


def _dot_tn(a, b):
    return lax.dot_general(a, b, (((0,), (0,)), ((), ())), preferred_element_type=F32)


def _rms(x, g):
    ms = jnp.mean(x * x, axis=-1, keepdims=True)
    return x * lax.rsqrt(ms + EPS) * g


def _sigmoid(x):
    return 1.0 / (1.0 + jnp.exp(-x))


def _group_mean_sq(x, group_ones, inv_n):
    x2 = x * x
    hi = x2.astype(BF16)
    lo = (x2 - hi.astype(F32)).astype(BF16)
    return (_dot(hi, group_ones) + _dot(lo, group_ones)) * inv_n


def _group_ones(n, group):
    r = lax.broadcasted_iota(jnp.int32, (n, n), 0) // group
    c = lax.broadcasted_iota(jnp.int32, (n, n), 1) // group
    return jnp.where(r == c, 1.0, 0.0).astype(BF16)


def _rope_slab(x, cos_t, sin_lo, sin_hi):
    return (x * cos_t + pltpu.roll(x, LANES - ROT_DIM // 2, axis=1) * sin_lo
            + pltpu.roll(x, ROT_DIM // 2, axis=1) * sin_hi)


def _resident(shape):
    nd = len(shape)
    return pl.BlockSpec(shape, lambda *_: (0,) * nd, pipeline_mode=pl.Buffered(1))


def _inv_rms(x):
    return lax.rsqrt(jnp.mean(x * x, axis=-1, keepdims=True) + EPS)


def _swiglu_residual(x_ref, g, wgu_ref, wd_ref, o_ref, before=None):
    rows = FFN_TM // FFN_SPLIT
    groups = [slice(i * rows, (i + 1) * rows) for i in range(FFN_SPLIT)]
    inv = [_inv_rms(x_ref[r, :]) for r in groups]
    hns = [(x_ref[r, :] * inv[i] * g).astype(BF16) for i, r in enumerate(groups)]
    if before is not None:
        before(inv)
    accs = [None] * FFN_SPLIT
    for c in range(D_FF // FFN_CF):
        lo = c * FFN_CF
        for i, hn in enumerate(hns):
            gate = _dot(hn, wgu_ref[:, lo:lo + FFN_CF])
            up = _dot(hn, wgu_ref[:, D_FF + lo:D_FF + lo + FFN_CF])
            act = (gate * _sigmoid(gate) * up).astype(BF16)
            part = _dot(act, wd_ref[lo:lo + FFN_CF, :])
            accs[i] = part if accs[i] is None else accs[i] + part
    for r, acc in zip(groups, accs):
        o_ref[r, :] = x_ref[r, :] + MACARON_WEIGHT * acc


def _shared_kv(i, inv, x_ref, kvg_ref, kvw_ref, kg_ref, cos_ref, slo_ref, shi_ref, kv_ref):
    rows = FFN_TM // FFN_SPLIT
    r = slice(i * rows, (i + 1) * rows)
    hn = (x_ref[r, :] * inv * kvg_ref[...]).astype(BF16)
    kv = _dot(hn, kvw_ref[...])
    k = kv[:, :LANES]
    v = kv[:, LANES:]
    ms = _group_mean_sq(k, _group_ones(LANES, ATT_HEAD_DIM), 1.0 / ATT_HEAD_DIM)
    k = k * lax.rsqrt(ms + EPS) * kg_ref[...]
    k = _rope_slab(k, cos_ref[r, :], slo_ref[r, :], shi_ref[r, :])
    kv_ref[r, 0 * LANES:1 * LANES] = k.astype(BF16)
    kv_ref[r, 1 * LANES:2 * LANES] = pltpu.roll(k, ATT_HEAD_DIM, axis=1).astype(BF16)
    kv_ref[r, 2 * LANES:3 * LANES] = v.astype(BF16)
    kv_ref[r, 3 * LANES:4 * LANES] = pltpu.roll(v, ATT_HEAD_DIM, axis=1).astype(BF16)


def _ffn_kernel(*refs, with_kv, n_cast):
    n_in = 4 + (6 if with_kv else 0) + n_cast
    ins, outs = refs[:n_in], refs[n_in:]
    x_ref, g_ref, wgu_ref, wd_ref = ins[:4]

    def kv_first(inv):
        for i in range(FFN_SPLIT):
            _shared_kv(i, inv[i], x_ref, *ins[4:10], outs[1])

    _swiglu_residual(x_ref, g_ref[...], wgu_ref, wd_ref, outs[0],
                     before=kv_first if with_kv else None)
    for src_ref, dst_ref in zip(ins[n_in - n_cast:], outs[len(outs) - n_cast:]):
        dst_ref[...] = src_ref[...].astype(BF16)


def _cast_rows(n_rows, steps):
    rows = BF16_SUBLANES
    while n_rows % rows or n_rows // rows > steps:
        rows += BF16_SUBLANES
    return rows


def _ffn(h, g, w_gu, w_d, name, kv_args=None, casts=()):
    t = h.shape[0]
    steps = t // FFN_TM
    in_specs = [
        pl.BlockSpec((FFN_TM, D_MODEL), lambda i: (i, 0)),
        _resident((1, D_MODEL)),
        _resident((D_MODEL, 2 * D_FF)),
        _resident((D_FF, D_MODEL)),
    ]
    out_shape = [jax.ShapeDtypeStruct((t, D_MODEL), F32)]
    out_specs = [pl.BlockSpec((FFN_TM, D_MODEL), lambda i: (i, 0))]
    args = [h, g, w_gu, w_d]
    if kv_args is not None:
        seq = kv_args[-1]
        per_seq = seq // FFN_TM
        table = pl.BlockSpec((FFN_TM, LANES), lambda i: (i % per_seq, 0))
        in_specs += [_resident((1, D_MODEL)), _resident((D_MODEL, 2 * LANES)),
                     _resident((1, LANES)), table, table, table]
        out_shape.append(jax.ShapeDtypeStruct((t, KV_LANES), BF16))
        out_specs.append(pl.BlockSpec((FFN_TM, KV_LANES), lambda i: (i, 0)))
        args += list(kv_args[:-1])
    for w, lead in casts:
        n_rows, cols = w.shape[len(lead):]
        rows = _cast_rows(n_rows, steps)
        last = n_rows // rows - 1
        in_specs.append(pl.BlockSpec(
            (None,) * len(lead) + (rows, cols),
            lambda i, lead=lead, last=last: lead + (jnp.minimum(i, last), 0)))
        out_specs.append(pl.BlockSpec((rows, cols), lambda i, last=last: (jnp.minimum(i, last), 0)))
        out_shape.append(jax.ShapeDtypeStruct((n_rows, cols), BF16))
        args.append(w)
    return pl.pallas_call(
        functools.partial(_ffn_kernel, with_kv=kv_args is not None, n_cast=len(casts)),
        out_shape=tuple(out_shape),
        grid=(steps,),
        in_specs=in_specs,
        out_specs=tuple(out_specs),
        compiler_params=pltpu.CompilerParams(
            dimension_semantics=("arbitrary",), vmem_limit_bytes=VMEM_LIMIT),
        name=name,
    )(*args)


def _hgrn_kernel(h_ref, g_ref, win_ref, lbl_ref, og_ref, wout_ref, o_ref,
                 qe_ref, ke_ref, kd_ref, v_ref, gate_ref, y_ref, state_ref):
    n_chunks = HG_TS // HG_CHUNK
    seqs = range(HG_SEQS)

    @pl.when(pl.program_id(1) == 0)
    def _():
        state_ref[...] = jnp.zeros_like(state_ref)

    logits = lbl_ref[...]
    top = jnp.max(logits, axis=0, keepdims=True)
    e = jnp.exp(logits - top)
    lower = e[0:1, :] / jnp.sum(e, axis=0, keepdims=True)

    row = lax.broadcasted_iota(jnp.int32, (HG_TS, HG_TS), 0)
    col = lax.broadcasted_iota(jnp.int32, (HG_TS, HG_TS), 1)
    causal = (col <= row) & (col // HG_CHUNK == row // HG_CHUNK)
    tril = jnp.where(causal, 1.0, 0.0).astype(BF16)

    g = g_ref[...]
    hn = [_rms(h_ref[t], g).astype(BF16) for t in seqs]

    forget = [lower + (1.0 - lower) * _sigmoid(_dot(hn[t], win_ref[:, HG_FD:2 * HG_FD]))
              for t in seqs]
    k = [1.0 - forget[t] for t in seqs]
    log_f = [jnp.log(forget[t]) for t in seqs]
    hi = [log_f[t].astype(BF16) for t in seqs]
    lo = [(log_f[t] - hi[t].astype(F32)).astype(BF16) for t in seqs]
    b = [_dot(tril, hi[t]) + _dot(tril, lo[t]) for t in seqs]
    q = [_dot(hn[t], win_ref[:, 0:HG_FD]) for t in seqs]
    for t in seqs:
        qe_ref[t] = (q[t] * _sigmoid(q[t]) * jnp.exp(b[t])).astype(BF16)
        ke_ref[t] = (k[t] * jnp.exp(-b[t])).astype(BF16)
    for t in seqs:
        v_ref[t] = _dot(hn[t], win_ref[:, 2 * HG_FD:2 * HG_FD + D_MODEL]).astype(BF16)
    decay = [[] for _ in seqs]
    for t in seqs:
        for c in range(n_chunks):
            rs = slice(c * HG_CHUNK, (c + 1) * HG_CHUNK)
            b_end = b[t][(c + 1) * HG_CHUNK - 1:(c + 1) * HG_CHUNK, :]
            kd_ref[t, rs, :] = (k[t][rs, :] * jnp.exp(b_end - b[t][rs, :])).astype(BF16)
            decay[t].append(jnp.exp(b_end))
    for t in seqs:
        gate = _dot(hn[t], win_ref[:, 2 * HG_FD + D_MODEL:])
        gate_ref[t] = gate * _sigmoid(gate)

    og = og_ref[...]
    heads = range(HG_HEADS)
    ks = [slice(hd * HG_DK, (hd + 1) * HG_DK) for hd in heads]
    vs = [slice(hd * HG_DV, (hd + 1) * HG_DV) for hd in heads]
    pad = jnp.zeros((HG_CHUNK, HG_DK), BF16)
    o_intra = [[None] * HG_HEADS for _ in seqs]
    incr = [[[None] * n_chunks for _ in heads] for _ in seqs]

    def within_chunks(t):
        scores = []
        for hd in heads:
            scores.append(jnp.where(causal, _dot_nt(qe_ref[t, :, ks[hd]], ke_ref[t, :, ks[hd]]),
                                    0.0).astype(BF16))
            yield
        for hd in heads:
            o_intra[t][hd] = _dot(scores[hd], v_ref[t, :, vs[hd]])
            yield
        for c in range(0, n_chunks, 2):
            r0 = slice(c * HG_CHUNK, (c + 1) * HG_CHUNK)
            r1 = slice((c + 1) * HG_CHUNK, (c + 2) * HG_CHUNK)
            both = slice(c * HG_CHUNK, (c + 2) * HG_CHUNK)
            for hd in heads:
                kd_pair = jnp.concatenate(
                    [jnp.concatenate([kd_ref[t, r0, ks[hd]], pad], axis=1),
                     jnp.concatenate([pad, kd_ref[t, r1, ks[hd]]], axis=1)], axis=0)
                pair = _dot_tn(v_ref[t, both, vs[hd]], kd_pair)
                incr[t][hd][c] = pair[:, :HG_DK]
                incr[t][hd][c + 1] = pair[:, HG_DK:]
                yield

    def across_chunks(t):
        st = [state_ref[t, hd] for hd in heads]
        o_parts = [[] for _ in heads]
        for c in range(n_chunks):
            rs = slice(c * HG_CHUNK, (c + 1) * HG_CHUNK)
            for hd in heads:
                o_parts[hd].append(o_intra[t][hd][rs, :]
                                   + _dot(qe_ref[t, rs, ks[hd]], st[hd].T.astype(BF16)))
                yield
            for hd in heads:
                st[hd] = decay[t][c][:, ks[hd]] * st[hd] + incr[t][hd][c]
        for hd in heads:
            state_ref[t, hd] = st[hd]
            o = jnp.concatenate(o_parts[hd], axis=0)
            y_ref[t, :, vs[hd]] = (_rms(o, og) * gate_ref[t, :, vs[hd]]).astype(BF16)
            yield

    def project_out(t):
        o_ref[t] = h_ref[t] + _dot(y_ref[t], wout_ref[...])
        yield

    def emit_together(*stages):
        live = list(stages)
        while live:
            for stage in list(live):
                if next(stage, _DONE) is _DONE:
                    live.remove(stage)

    emit_together(within_chunks(0))
    for t in range(1, HG_SEQS + 2):
        emit_together(*[stage(t - lag) for lag, stage in
                        enumerate((within_chunks, across_chunks, project_out))
                        if 0 <= t - lag < HG_SEQS])


def _hgrn(h, g, w_in, lb_logits, onorm_g, w_out):
    batch, seq, _ = h.shape
    steps = seq // HG_TS
    n_lb = lb_logits.shape[0]
    tile = pl.BlockSpec((HG_SEQS, HG_TS, D_MODEL), lambda b, s: (b, s, 0))
    return pl.pallas_call(
        _hgrn_kernel,
        out_shape=jax.ShapeDtypeStruct(h.shape, F32),
        grid=(batch // HG_SEQS, steps),
        in_specs=[
            tile,
            _resident((1, D_MODEL)),
            _resident((D_MODEL, 2 * HG_FD + 2 * D_MODEL)),
            _resident((n_lb, HG_FD)),
            _resident((1, HG_DV)),
            _resident((D_MODEL, D_MODEL)),
        ],
        out_specs=tile,
        scratch_shapes=[
            pltpu.VMEM((HG_SEQS, HG_TS, HG_FD), BF16),
            pltpu.VMEM((HG_SEQS, HG_TS, HG_FD), BF16),
            pltpu.VMEM((HG_SEQS, HG_TS, HG_FD), BF16),
            pltpu.VMEM((HG_SEQS, HG_TS, D_MODEL), BF16),
            pltpu.VMEM((HG_SEQS, HG_TS, D_MODEL), F32),
            pltpu.VMEM((HG_SEQS, HG_TS, D_MODEL), BF16),
            pltpu.VMEM((HG_SEQS, HG_HEADS, HG_DV, HG_DK), F32),
        ],
        compiler_params=pltpu.CompilerParams(
            dimension_semantics=("parallel", "arbitrary"), vmem_limit_bytes=VMEM_LIMIT),
        name="hgrn",
    )(h, g, w_in, lb_logits, onorm_g, w_out)


def _swa_kernel(sink_ref, h_ref, g_ref, wq_ref, qg_ref, cos_ref, sin_ref,
                halo_ref, kv_ref, wo_ref, o_ref, q_buf, kvar, vvar, o_buf):
    x = h_ref[...]
    hn = _rms(x, g_ref[...]).astype(BF16)
    q = _dot(hn, wq_ref[...])
    wide = 2 * LANES
    ones = _group_ones(wide, ATT_HEAD_DIM)
    src = lax.broadcasted_iota(jnp.int32, (wide, wide), 0)
    dst = lax.broadcasted_iota(jnp.int32, (wide, wide), 1)
    dim = dst % ATT_HEAD_DIM
    half_rot = ROT_DIM // 2
    partner = jnp.where((dim < half_rot) & (src == dst + half_rot), -1.0,
                        jnp.where((dim >= half_rot) & (dim < ROT_DIM) & (src == dst - half_rot),
                                  1.0, 0.0)).astype(BF16)
    qg = jnp.concatenate([qg_ref[...]] * 2, axis=1)
    cos_t = jnp.concatenate([cos_ref[...]] * 2, axis=1)
    sin_t = jnp.concatenate([sin_ref[...]] * 2, axis=1)
    for s2 in range(Q_SLABS // 2):
        qq = q[:, s2 * wide:(s2 + 1) * wide]
        ms = _dot((qq * qq).astype(BF16), ones) * (1.0 / ATT_HEAD_DIM)
        qn = qq * lax.rsqrt(ms + EPS) * qg
        qs = qn * cos_t + _dot(qn.astype(BF16), partner) * sin_t
        q_buf[:, s2 * wide:(s2 + 1) * wide] = (qs * (ATT_SCALE * LOG2E)).astype(BF16)

    kv_all = jnp.concatenate([halo_ref[...], kv_ref[...]], axis=0)
    k_plain, k_swap = kv_all[:, 0:LANES], kv_all[:, LANES:2 * LANES]
    v_plain, v_swap = kv_all[:, 2 * LANES:3 * LANES], kv_all[:, 3 * LANES:]
    low = lax.broadcasted_iota(jnp.int32, k_plain.shape, 1) < ATT_HEAD_DIM
    zero = jnp.zeros_like(k_plain)
    ones_lo = jnp.where(low, 1.0, 0.0).astype(BF16)
    ones_hi = jnp.where(low, 0.0, 1.0).astype(BF16)
    k_src = ((k_plain, True), (k_swap, False), (k_swap, True), (k_plain, False))
    v_src = ((v_plain, True), (v_swap, False), (v_swap, True), (v_plain, False))
    for i, ((kt, in_low), (vt, _)) in enumerate(zip(k_src, v_src)):
        kvar[:, i * LANES:(i + 1) * LANES] = jnp.where(low, kt, zero) if in_low else jnp.where(low, zero, kt)
        vv = jnp.where(low, vt, zero) if in_low else jnp.where(low, zero, vt)
        vvar[:, 2 * i * LANES:2 * (i + 1) * LANES] = jnp.concatenate(
            [vv, ones_lo if in_low else ones_hi], axis=1)

    cur_ok = (lax.broadcasted_iota(jnp.int32, (WINDOW, WINDOW), 1)
              <= lax.broadcasted_iota(jnp.int32, (WINDOW, WINDOW), 0))
    low_q = lax.broadcasted_iota(jnp.int32, (WINDOW, LANES), 1) < ATT_HEAD_DIM
    at_seq_start = pl.program_id(1) == 0

    def block(qb, carry):
        r0 = pl.multiple_of(qb * WINDOW, WINDOW)
        rows = pl.ds(r0, WINDOW)
        band = pl.ds(r0, 2 * WINDOW)
        prev_bias = jnp.where(jnp.logical_and(at_seq_start, qb == 0), MASKED, 0.0)
        q_st = [jnp.concatenate(
            [q_buf[rows, (kvh * SLABS_PER_KV + p) * LANES:(kvh * SLABS_PER_KV + p + 1) * LANES]
             for p in range(SLABS_PER_KV)], axis=0) for kvh in range(ATT_KV_HEADS)]
        acc = [None] * ATT_KV_HEADS
        sink_e = [[] for _ in range(ATT_KV_HEADS)]
        for par in range(HEADS_PER_SLAB):
            s_all = [_dot_nt(q_st[kvh], kvar[band, (kvh * HEADS_PER_SLAB + par) * LANES:
                                             (kvh * HEADS_PER_SLAB + par + 1) * LANES])
                     for kvh in range(ATT_KV_HEADS)]
            for kvh in range(ATT_KV_HEADS):
                slab = kvh * HEADS_PER_SLAB + par
                es = []
                se = []
                for p in range(SLABS_PER_KV):
                    head = kvh * ATT_GROUP + p * HEADS_PER_SLAB + par
                    sink = sink_ref[head] * LOG2E
                    pr = slice(p * WINDOW, (p + 1) * WINDOW)
                    z = jnp.where(cur_ok, s_all[kvh][pr, WINDOW:], s_all[kvh][pr, 0:WINDOW] + prev_bias)
                    m = jnp.maximum(jnp.max(z, axis=1, keepdims=True), sink)
                    ex = jnp.exp2(z - m)
                    es.append(jnp.concatenate(
                        [jnp.where(cur_ok, 0.0, ex), jnp.where(cur_ok, ex, 0.0)],
                        axis=1).astype(BF16))
                    se.append(jnp.exp2(sink - m))
                e_all = jnp.concatenate(es, axis=0)
                part = _dot(e_all, vvar[band, 2 * slab * LANES:2 * (slab + 1) * LANES])
                acc[kvh] = part if acc[kvh] is None else acc[kvh] + part
                sink_e[kvh].append(se)
        for kvh in range(ATT_KV_HEADS):
            for p in range(SLABS_PER_KV):
                pr = slice(p * WINDOW, (p + 1) * WINDOW)
                den = acc[kvh][pr, LANES:] + jnp.where(low_q, sink_e[kvh][0][p], sink_e[kvh][1][p])
                sl = (kvh * SLABS_PER_KV + p) * LANES
                o_buf[rows, sl:sl + LANES] = (acc[kvh][pr, :LANES] / den).astype(BF16)
        return carry

    lax.fori_loop(0, ATT_TQ // WINDOW, block, 0, unroll=True)
    o_ref[...] = x + _dot(o_buf[...], wo_ref[...])


def _swa(h, g, w_q, qg, sinks, cos_t, sin_t, kv, w_o, batch, seq):
    steps = seq // ATT_TQ
    blocks_per_step = ATT_TQ // WINDOW
    blocks_per_seq = seq // WINDOW

    def halo_map(b, s):
        return (jnp.maximum(b * blocks_per_seq + s * blocks_per_step - 1, 0), 0)

    return pl.pallas_call(
        _swa_kernel,
        out_shape=jax.ShapeDtypeStruct(h.shape, F32),
        grid=(batch, steps),
        in_specs=[
            pl.BlockSpec(memory_space=pltpu.SMEM),
            pl.BlockSpec((ATT_TQ, D_MODEL), lambda b, s: (b * steps + s, 0)),
            _resident((1, D_MODEL)),
            _resident((D_MODEL, D_MODEL)),
            _resident((1, LANES)),
            pl.BlockSpec((ATT_TQ, LANES), lambda b, s: (s, 0)),
            pl.BlockSpec((ATT_TQ, LANES), lambda b, s: (s, 0)),
            pl.BlockSpec((WINDOW, KV_LANES), halo_map),
            pl.BlockSpec((ATT_TQ, KV_LANES), lambda b, s: (b * steps + s, 0)),
            _resident((D_MODEL, D_MODEL)),
        ],
        out_specs=pl.BlockSpec((ATT_TQ, D_MODEL), lambda b, s: (b * steps + s, 0)),
        scratch_shapes=[
            pltpu.VMEM((ATT_TQ, D_MODEL), BF16),
            pltpu.VMEM((ATT_TQ + WINDOW, 4 * LANES), BF16),
            pltpu.VMEM((ATT_TQ + WINDOW, 8 * LANES), BF16),
            pltpu.VMEM((ATT_TQ, D_MODEL), BF16),
        ],
        compiler_params=pltpu.CompilerParams(
            dimension_semantics=("parallel", "parallel"), vmem_limit_bytes=VMEM_LIMIT),
        name="swa",
    )(sinks, h, g, w_q, qg, cos_t, sin_t, kv, kv, w_o)


def _rope_tables(seq):
    half = ROT_DIM // 2
    inv_freq = jnp.power(ROPE_THETA, -jnp.arange(half, dtype=F32) * (2.0 / ROT_DIM))
    ang = jnp.arange(seq).astype(F32)[:, None] * inv_freq[None, :]
    cos, sin = jnp.cos(ang), jnp.sin(ang)
    zeros = jnp.zeros_like(sin)
    rest = ATT_HEAD_DIM - ROT_DIM
    pad0 = jnp.zeros((seq, rest), F32)
    cos_t = jnp.concatenate([cos, cos, jnp.ones((seq, rest), F32)], axis=1)
    sin_lo = jnp.concatenate([-sin, zeros, pad0], axis=1)
    sin_hi = jnp.concatenate([zeros, sin, pad0], axis=1)
    sin_t = jnp.concatenate([sin, sin, pad0], axis=1)
    return tuple(jnp.tile(t, (1, HEADS_PER_SLAB)) for t in (cos_t, sin_lo, sin_hi, sin_t))


def kernel(x, ffn_norm_g, ffn_w_gate_up, ffn_w_down, mix_norm_g, hgrn_w_in, hgrn_lb_logits,
           hgrn_onorm_g, hgrn_w_out, kv_norm_g, kv_w, k_norm_g, attn_w_q, q_norm_g,
           attn_sinks, attn_w_out):
    batch, seq, d = x.shape
    assert d == D_MODEL and seq % ATT_TQ == 0 and seq % HG_TS == 0 and batch % HG_SEQS == 0
    assert (batch * seq) % FFN_TM == 0 and seq % FFN_TM == 0
    h = x.reshape(batch * seq, d)

    cos_t, sin_lo, sin_hi, sin_t = _rope_tables(seq)
    ffn_g = ffn_norm_g.reshape(ffn_norm_g.shape[0], 2, 1, D_MODEL)

    h, w_gu01, w_d01, w_in, w_hout = _ffn(
        h, ffn_g[0, 0], ffn_w_gate_up[0, 0].astype(BF16), ffn_w_down[0, 0].astype(BF16), "ffn_0_0",
        casts=((ffn_w_gate_up, (0, 1)), (ffn_w_down, (0, 1)), (hgrn_w_in, (0,)), (hgrn_w_out, (0,))))
    h = _hgrn(h.reshape(batch, seq, d), mix_norm_g[0].reshape(1, D_MODEL), w_in,
              hgrn_lb_logits, hgrn_onorm_g[0].reshape(1, HG_DV), w_hout).reshape(batch * seq, d)
    h, w_gu10, w_d10, w_kv, w_q, w_aout = _ffn(
        h, ffn_g[0, 1], w_gu01, w_d01, "ffn_0_1",
        casts=((ffn_w_gate_up, (1, 0)), (ffn_w_down, (1, 0)), (kv_w, ()), (attn_w_q, (0,)),
               (attn_w_out, (0,))))
    h, kv, w_gu11, w_d11 = _ffn(
        h, ffn_g[1, 0], w_gu10, w_d10, "ffn_1_0",
        kv_args=(kv_norm_g.reshape(1, D_MODEL), w_kv,
                 jnp.tile(k_norm_g, HEADS_PER_SLAB).reshape(1, LANES),
                 cos_t, sin_lo, sin_hi, seq),
        casts=((ffn_w_gate_up, (1, 1)), (ffn_w_down, (1, 1))))
    h = _swa(h, mix_norm_g[1].reshape(1, D_MODEL), w_q,
             jnp.tile(q_norm_g[0], HEADS_PER_SLAB).reshape(1, LANES), attn_sinks[0],
             cos_t, sin_t, kv, w_aout, batch, seq)
    (h,) = _ffn(h, ffn_g[1, 1], w_gu11, w_d11, "ffn_1_1")
    return h.reshape(batch, seq, d)
```

```python
import functools

import jax
import jax.numpy as jnp
from jax import lax
from jax.experimental import pallas as pl
from jax.experimental.pallas import tpu as pltpu

D_MODEL = 1024
D_FF = 2816
MACARON_WEIGHT = 0.5

HG_HEADS = 8
HG_DK = 128
HG_FD = HG_HEADS * HG_DK
HG_DV = D_MODEL // HG_HEADS
HG_CHUNK = 64

ATT_HEADS = 16
ATT_KV_HEADS = 2
ATT_GROUP = ATT_HEADS // ATT_KV_HEADS
ATT_HEAD_DIM = 64
WINDOW = 128
ATT_SCALE = ATT_HEAD_DIM ** -0.5
ROPE_THETA = 500000.0
ROT_DIM = ATT_HEAD_DIM // 4
EPS = 1e-6

LANES = 128
BF16_SUBLANES = 16
HEADS_PER_SLAB = LANES // ATT_HEAD_DIM
Q_SLABS = ATT_HEADS // HEADS_PER_SLAB
SLABS_PER_KV = Q_SLABS // ATT_KV_HEADS
MASKED = -1e30
LOG2E = 1.4426950408889634
KV_LANES = 4 * LANES

FFN_TM = 1024
FFN_SPLIT = 2
FFN_CF = 256
HG_TS = 256
HG_SEQS = 4
ATT_TQ = 1024
VMEM_LIMIT = 56 * 1024 * 1024

BF16 = jnp.bfloat16
F32 = jnp.float32
_DONE = object()


def _dot(a, b):
    return jnp.dot(a, b, preferred_element_type=F32)


def _dot_nt(a, b):
    return lax.dot_general(a, b, (((1,), (1,)), ((), ())), preferred_element_type=F32)


def _dot_tn(a, b):
    return lax.dot_general(a, b, (((0,), (0,)), ((), ())), preferred_element_type=F32)


def _rms(x, g):
    ms = jnp.mean(x * x, axis=-1, keepdims=True)
    return x * lax.rsqrt(ms + EPS) * g


def _sigmoid(x):
    return 1.0 / (1.0 + jnp.exp(-x))


def _group_mean_sq(x, group_ones, inv_n):
    x2 = x * x
    hi = x2.astype(BF16)
    lo = (x2 - hi.astype(F32)).astype(BF16)
    return (_dot(hi, group_ones) + _dot(lo, group_ones)) * inv_n


def _group_ones(n, group):
    r = lax.broadcasted_iota(jnp.int32, (n, n), 0) // group
    c = lax.broadcasted_iota(jnp.int32, (n, n), 1) // group
    return jnp.where(r == c, 1.0, 0.0).astype(BF16)


def _rope_slab(x, cos_t, sin_lo, sin_hi):
    return (x * cos_t + pltpu.roll(x, LANES - ROT_DIM // 2, axis=1) * sin_lo
            + pltpu.roll(x, ROT_DIM // 2, axis=1) * sin_hi)


def _resident(shape):
    nd = len(shape)
    return pl.BlockSpec(shape, lambda *_: (0,) * nd, pipeline_mode=pl.Buffered(1))


def _inv_rms(x):
    return lax.rsqrt(jnp.mean(x * x, axis=-1, keepdims=True) + EPS)


def _swiglu_residual(x_ref, g, wgu_ref, wd_ref, o_ref, before=None):
    rows = FFN_TM // FFN_SPLIT
    groups = [slice(i * rows, (i + 1) * rows) for i in range(FFN_SPLIT)]
    inv = [_inv_rms(x_ref[r, :]) for r in groups]
    hns = [(x_ref[r, :] * inv[i] * g).astype(BF16) for i, r in enumerate(groups)]
    if before is not None:
        before(inv)
    accs = [None] * FFN_SPLIT
    for c in range(D_FF // FFN_CF):
        lo = c * FFN_CF
        for i, hn in enumerate(hns):
            gate = _dot(hn, wgu_ref[:, lo:lo + FFN_CF])
            up = _dot(hn, wgu_ref[:, D_FF + lo:D_FF + lo + FFN_CF])
            act = (gate * _sigmoid(gate) * up).astype(BF16)
            part = _dot(act, wd_ref[lo:lo + FFN_CF, :])
            accs[i] = part if accs[i] is None else accs[i] + part
    for r, acc in zip(groups, accs):
        o_ref[r, :] = x_ref[r, :] + MACARON_WEIGHT * acc


def _shared_kv(i, inv, x_ref, kvg_ref, kvw_ref, kg_ref, cos_ref, slo_ref, shi_ref, kv_ref):
    rows = FFN_TM // FFN_SPLIT
    r = slice(i * rows, (i + 1) * rows)
    hn = (x_ref[r, :] * inv * kvg_ref[...]).astype(BF16)
    kv = _dot(hn, kvw_ref[...])
    k = kv[:, :LANES]
    v = kv[:, LANES:]
    ms = _group_mean_sq(k, _group_ones(LANES, ATT_HEAD_DIM), 1.0 / ATT_HEAD_DIM)
    k = k * lax.rsqrt(ms + EPS) * kg_ref[...]
    k = _rope_slab(k, cos_ref[r, :], slo_ref[r, :], shi_ref[r, :])
    kv_ref[r, 0 * LANES:1 * LANES] = k.astype(BF16)
    kv_ref[r, 1 * LANES:2 * LANES] = pltpu.roll(k, ATT_HEAD_DIM, axis=1).astype(BF16)
    kv_ref[r, 2 * LANES:3 * LANES] = v.astype(BF16)
    kv_ref[r, 3 * LANES:4 * LANES] = pltpu.roll(v, ATT_HEAD_DIM, axis=1).astype(BF16)


def _ffn_kernel(*refs, with_kv, n_cast):
    n_in = 4 + (6 if with_kv else 0) + n_cast
    ins, outs = refs[:n_in], refs[n_in:]
    x_ref, g_ref, wgu_ref, wd_ref = ins[:4]
    for src_ref, dst_ref in zip(ins[n_in - n_cast:], outs[len(outs) - n_cast:]):
        dst_ref[...] = src_ref[...].astype(BF16)

    def kv_first(inv):
        for i in range(FFN_SPLIT):
            _shared_kv(i, inv[i], x_ref, *ins[4:10], outs[1])

    _swiglu_residual(x_ref, g_ref[...], wgu_ref, wd_ref, outs[0],
                     before=kv_first if with_kv else None)


def _cast_rows(n_rows, steps):
    rows = BF16_SUBLANES
    while n_rows % rows or n_rows // rows > steps:
        rows += BF16_SUBLANES
    return rows


def _ffn(h, g, w_gu, w_d, name, kv_args=None, casts=()):
    t = h.shape[0]
    steps = t // FFN_TM
    in_specs = [
        pl.BlockSpec((FFN_TM, D_MODEL), lambda i: (i, 0)),
        _resident((1, D_MODEL)),
        _resident((D_MODEL, 2 * D_FF)),
        _resident((D_FF, D_MODEL)),
    ]
    out_shape = [jax.ShapeDtypeStruct((t, D_MODEL), F32)]
    out_specs = [pl.BlockSpec((FFN_TM, D_MODEL), lambda i: (i, 0))]
    args = [h, g, w_gu, w_d]
    if kv_args is not None:
        seq = kv_args[-1]
        per_seq = seq // FFN_TM
        table = pl.BlockSpec((FFN_TM, LANES), lambda i: (i % per_seq, 0))
        in_specs += [_resident((1, D_MODEL)), _resident((D_MODEL, 2 * LANES)),
                     _resident((1, LANES)), table, table, table]
        out_shape.append(jax.ShapeDtypeStruct((t, KV_LANES), BF16))
        out_specs.append(pl.BlockSpec((FFN_TM, KV_LANES), lambda i: (i, 0)))
        args += list(kv_args[:-1])
    for w, lead in casts:
        n_rows, cols = w.shape[len(lead):]
        rows = _cast_rows(n_rows, steps)
        last = n_rows // rows - 1
        in_specs.append(pl.BlockSpec(
            (None,) * len(lead) + (rows, cols),
            lambda i, lead=lead, last=last: lead + (jnp.minimum(i, last), 0)))
        out_specs.append(pl.BlockSpec((rows, cols), lambda i, last=last: (jnp.minimum(i, last), 0)))
        out_shape.append(jax.ShapeDtypeStruct((n_rows, cols), BF16))
        args.append(w)
    return pl.pallas_call(
        functools.partial(_ffn_kernel, with_kv=kv_args is not None, n_cast=len(casts)),
        out_shape=tuple(out_shape),
        grid=(steps,),
        in_specs=in_specs,
        out_specs=tuple(out_specs),
        compiler_params=pltpu.CompilerParams(
            dimension_semantics=("arbitrary",), vmem_limit_bytes=VMEM_LIMIT),
        name=name,
    )(*args)


def _hgrn_kernel(h_ref, g_ref, win_ref, lbl_ref, og_ref, wout_ref, o_ref,
                 qe_ref, ke_ref, kd_ref, v_ref, gate_ref, y_ref, state_ref):
    n_chunks = HG_TS // HG_CHUNK
    seqs = range(HG_SEQS)

    @pl.when(pl.program_id(1) == 0)
    def _():
        state_ref[...] = jnp.zeros_like(state_ref)

    logits = lbl_ref[...]
    top = jnp.max(logits, axis=0, keepdims=True)
    e = jnp.exp(logits - top)
    lower = e[0:1, :] / jnp.sum(e, axis=0, keepdims=True)

    row = lax.broadcasted_iota(jnp.int32, (HG_TS, HG_TS), 0)
    col = lax.broadcasted_iota(jnp.int32, (HG_TS, HG_TS), 1)
    causal = (col <= row) & (col // HG_CHUNK == row // HG_CHUNK)
    tril = jnp.where(causal, 1.0, 0.0).astype(BF16)

    g = g_ref[...]
    hn = [_rms(h_ref[t], g).astype(BF16) for t in seqs]

    halves = [slice(i * HG_FD // 2, (i + 1) * HG_FD // 2) for i in range(2)]
    decay_parts = [[[] for _ in range(n_chunks)] for _ in seqs]
    for fh in halves:
        f_cols = slice(HG_FD + fh.start, HG_FD + fh.stop)
        low_h = lower[:, fh]
        forget = [low_h + (1.0 - low_h) * _sigmoid(_dot(hn[t], win_ref[:, f_cols])) for t in seqs]
        k = [1.0 - forget[t] for t in seqs]
        log_f = [jnp.log(forget[t]) for t in seqs]
        hi = [log_f[t].astype(BF16) for t in seqs]
        lo = [(log_f[t] - hi[t].astype(F32)).astype(BF16) for t in seqs]
        b = [_dot(tril, hi[t]) + _dot(tril, lo[t]) for t in seqs]
        q = [_dot(hn[t], win_ref[:, fh]) for t in seqs]
        for t in seqs:
            qe_ref[t, :, fh] = (q[t] * _sigmoid(q[t]) * jnp.exp(b[t])).astype(BF16)
            ke_ref[t, :, fh] = (k[t] * jnp.exp(-b[t])).astype(BF16)
        for t in seqs:
            for c in range(n_chunks):
                rs = slice(c * HG_CHUNK, (c + 1) * HG_CHUNK)
                b_end = b[t][(c + 1) * HG_CHUNK - 1:(c + 1) * HG_CHUNK, :]
                kd_ref[t, rs, fh] = (k[t][rs, :] * jnp.exp(b_end - b[t][rs, :])).astype(BF16)
                decay_parts[t][c].append(jnp.exp(b_end))
    decay = [[jnp.concatenate(decay_parts[t][c], axis=1) for c in range(n_chunks)] for t in seqs]
    for t in seqs:
        v_ref[t] = _dot(hn[t], win_ref[:, 2 * HG_FD:2 * HG_FD + D_MODEL]).astype(BF16)
    for t in seqs:
        gate = _dot(hn[t], win_ref[:, 2 * HG_FD + D_MODEL:])
        gate_ref[t] = gate * _sigmoid(gate)

    og = og_ref[...]
    heads = range(HG_HEADS)
    ks = [slice(hd * HG_DK, (hd + 1) * HG_DK) for hd in heads]
    vs = [slice(hd * HG_DV, (hd + 1) * HG_DV) for hd in heads]
    pad = jnp.zeros((HG_CHUNK, HG_DK), BF16)
    o_intra = [[None] * HG_HEADS for _ in seqs]
    incr = [[[None] * n_chunks for _ in heads] for _ in seqs]

    def within_chunks(t):
        scores = []
        for hd in heads:
            scores.append(jnp.where(causal, _dot_nt(qe_ref[t, :, ks[hd]], ke_ref[t, :, ks[hd]]),
                                    0.0).astype(BF16))
            yield
        for hd in heads:
            o_intra[t][hd] = _dot(scores[hd], v_ref[t, :, vs[hd]])
            yield
        for c in range(0, n_chunks, 2):
            r0 = slice(c * HG_CHUNK, (c + 1) * HG_CHUNK)
            r1 = slice((c + 1) * HG_CHUNK, (c + 2) * HG_CHUNK)
            both = slice(c * HG_CHUNK, (c + 2) * HG_CHUNK)
            for hd in heads:
                kd_pair = jnp.concatenate(
                    [jnp.concatenate([kd_ref[t, r0, ks[hd]], pad], axis=1),
                     jnp.concatenate([pad, kd_ref[t, r1, ks[hd]]], axis=1)], axis=0)
                pair = _dot_tn(v_ref[t, both, vs[hd]], kd_pair)
                incr[t][hd][c] = pair[:, :HG_DK]
                incr[t][hd][c + 1] = pair[:, HG_DK:]
                yield

    def across_chunks(t):
        st = [state_ref[t, hd] for hd in heads]
        o_parts = [[] for _ in heads]
        for c in range(n_chunks):
            rs = slice(c * HG_CHUNK, (c + 1) * HG_CHUNK)
            for hd in heads:
                o_parts[hd].append(o_intra[t][hd][rs, :]
                                   + _dot(qe_ref[t, rs, ks[hd]], st[hd].T.astype(BF16)))
                yield
            for hd in heads:
                st[hd] = decay[t][c][:, ks[hd]] * st[hd] + incr[t][hd][c]
        for hd in heads:
            state_ref[t, hd] = st[hd]
            o = jnp.concatenate(o_parts[hd], axis=0)
            y_ref[t, :, vs[hd]] = (_rms(o, og) * gate_ref[t, :, vs[hd]]).astype(BF16)
            yield

    def project_out(t):
        o_ref[t] = h_ref[t] + _dot(y_ref[t], wout_ref[...])
        yield

    def emit_together(*stages):
        live = list(stages)
        while live:
            for stage in list(live):
                if next(stage, _DONE) is _DONE:
                    live.remove(stage)

    emit_together(within_chunks(0))
    for t in range(1, HG_SEQS + 2):
        emit_together(*[stage(t - lag) for lag, stage in
                        enumerate((within_chunks, across_chunks, project_out))
                        if 0 <= t - lag < HG_SEQS])


def _hgrn(h, g, w_in, lb_logits, onorm_g, w_out):
    batch, seq, _ = h.shape
    steps = seq // HG_TS
    n_lb = lb_logits.shape[0]
    tile = pl.BlockSpec((HG_SEQS, HG_TS, D_MODEL), lambda b, s: (b, s, 0))
    return pl.pallas_call(
        _hgrn_kernel,
        out_shape=jax.ShapeDtypeStruct(h.shape, F32),
        grid=(batch // HG_SEQS, steps),
        in_specs=[
            tile,
            _resident((1, D_MODEL)),
            _resident((D_MODEL, 2 * HG_FD + 2 * D_MODEL)),
            _resident((n_lb, HG_FD)),
            _resident((1, HG_DV)),
            _resident((D_MODEL, D_MODEL)),
        ],
        out_specs=tile,
        scratch_shapes=[
            pltpu.VMEM((HG_SEQS, HG_TS, HG_FD), BF16),
            pltpu.VMEM((HG_SEQS, HG_TS, HG_FD), BF16),
            pltpu.VMEM((HG_SEQS, HG_TS, HG_FD), BF16),
            pltpu.VMEM((HG_SEQS, HG_TS, D_MODEL), BF16),
            pltpu.VMEM((HG_SEQS, HG_TS, D_MODEL), F32),
            pltpu.VMEM((HG_SEQS, HG_TS, D_MODEL), BF16),
            pltpu.VMEM((HG_SEQS, HG_HEADS, HG_DV, HG_DK), F32),
        ],
        compiler_params=pltpu.CompilerParams(
            dimension_semantics=("parallel", "arbitrary"), vmem_limit_bytes=VMEM_LIMIT),
        name="hgrn",
    )(h, g, w_in, lb_logits, onorm_g, w_out)


def _swa_kernel(sink_ref, h_ref, g_ref, wq_ref, qg_ref, cos_ref, sin_ref,
                halo_ref, kv_ref, wo_ref, o_ref, q_buf, kvar, vvar, o_buf):
    x = h_ref[...]
    hn = _rms(x, g_ref[...]).astype(BF16)
    q = _dot(hn, wq_ref[...])
    wide = 2 * LANES
    ones = _group_ones(wide, ATT_HEAD_DIM)
    src = lax.broadcasted_iota(jnp.int32, (wide, wide), 0)
    dst = lax.broadcasted_iota(jnp.int32, (wide, wide), 1)
    dim = dst % ATT_HEAD_DIM
    half_rot = ROT_DIM // 2
    partner = jnp.where((dim < half_rot) & (src == dst + half_rot), -1.0,
                        jnp.where((dim >= half_rot) & (dim < ROT_DIM) & (src == dst - half_rot),
                                  1.0, 0.0)).astype(BF16)
    qg = jnp.concatenate([qg_ref[...]] * 2, axis=1)
    cos_t = jnp.concatenate([cos_ref[...]] * 2, axis=1)
    sin_t = jnp.concatenate([sin_ref[...]] * 2, axis=1)
    for s2 in range(Q_SLABS // 2):
        qq = q[:, s2 * wide:(s2 + 1) * wide]
        ms = _dot((qq * qq).astype(BF16), ones) * (1.0 / ATT_HEAD_DIM)
        qn = qq * lax.rsqrt(ms + EPS) * qg
        qs = qn * cos_t + _dot(qn.astype(BF16), partner) * sin_t
        q_buf[:, s2 * wide:(s2 + 1) * wide] = (qs * (ATT_SCALE * LOG2E)).astype(BF16)

    kv_all = jnp.concatenate([halo_ref[...], kv_ref[...]], axis=0)
    k_plain, k_swap = kv_all[:, 0:LANES], kv_all[:, LANES:2 * LANES]
    v_plain, v_swap = kv_all[:, 2 * LANES:3 * LANES], kv_all[:, 3 * LANES:]
    low = lax.broadcasted_iota(jnp.int32, k_plain.shape, 1) < ATT_HEAD_DIM
    zero = jnp.zeros_like(k_plain)
    ones_lo = jnp.where(low, 1.0, 0.0).astype(BF16)
    ones_hi = jnp.where(low, 0.0, 1.0).astype(BF16)
    k_src = ((k_plain, True), (k_swap, False), (k_swap, True), (k_plain, False))
    v_src = ((v_plain, True), (v_swap, False), (v_swap, True), (v_plain, False))
    for i, ((kt, in_low), (vt, _)) in enumerate(zip(k_src, v_src)):
        kvar[:, i * LANES:(i + 1) * LANES] = jnp.where(low, kt, zero) if in_low else jnp.where(low, zero, kt)
        vv = jnp.where(low, vt, zero) if in_low else jnp.where(low, zero, vt)
        vvar[:, 2 * i * LANES:2 * (i + 1) * LANES] = jnp.concatenate(
            [vv, ones_lo if in_low else ones_hi], axis=1)

    cur_ok = (lax.broadcasted_iota(jnp.int32, (WINDOW, WINDOW), 1)
              <= lax.broadcasted_iota(jnp.int32, (WINDOW, WINDOW), 0))
    low_q = lax.broadcasted_iota(jnp.int32, (WINDOW, LANES), 1) < ATT_HEAD_DIM
    at_seq_start = pl.program_id(1) == 0

    def block(qb, carry):
        r0 = pl.multiple_of(qb * WINDOW, WINDOW)
        rows = pl.ds(r0, WINDOW)
        band = pl.ds(r0, 2 * WINDOW)
        prev_bias = jnp.where(jnp.logical_and(at_seq_start, qb == 0), MASKED, 0.0)
        q_st = [jnp.concatenate(
            [q_buf[rows, (kvh * SLABS_PER_KV + p) * LANES:(kvh * SLABS_PER_KV + p + 1) * LANES]
             for p in range(SLABS_PER_KV)], axis=0) for kvh in range(ATT_KV_HEADS)]
        acc = [None] * ATT_KV_HEADS
        sink_e = [[] for _ in range(ATT_KV_HEADS)]
        for par in range(HEADS_PER_SLAB):
            s_all = [_dot_nt(q_st[kvh], kvar[band, (kvh * HEADS_PER_SLAB + par) * LANES:
                                             (kvh * HEADS_PER_SLAB + par + 1) * LANES])
                     for kvh in range(ATT_KV_HEADS)]
            for kvh in range(ATT_KV_HEADS):
                slab = kvh * HEADS_PER_SLAB + par
                es = []
                se = []
                for p in range(SLABS_PER_KV):
                    head = kvh * ATT_GROUP + p * HEADS_PER_SLAB + par
                    sink = sink_ref[head] * LOG2E
                    pr = slice(p * WINDOW, (p + 1) * WINDOW)
                    z = jnp.where(cur_ok, s_all[kvh][pr, WINDOW:], s_all[kvh][pr, 0:WINDOW] + prev_bias)
                    m = jnp.maximum(jnp.max(z, axis=1, keepdims=True), sink)
                    ex = jnp.exp2(z - m)
                    es.append(jnp.concatenate(
                        [jnp.where(cur_ok, 0.0, ex), jnp.where(cur_ok, ex, 0.0)],
                        axis=1).astype(BF16))
                    se.append(jnp.exp2(sink - m))
                e_all = jnp.concatenate(es, axis=0)
                part = _dot(e_all, vvar[band, 2 * slab * LANES:2 * (slab + 1) * LANES])
                acc[kvh] = part if acc[kvh] is None else acc[kvh] + part
                sink_e[kvh].append(se)
        for kvh in range(ATT_KV_HEADS):
            for p in range(SLABS_PER_KV):
                pr = slice(p * WINDOW, (p + 1) * WINDOW)
                den = acc[kvh][pr, LANES:] + jnp.where(low_q, sink_e[kvh][0][p], sink_e[kvh][1][p])
                sl = (kvh * SLABS_PER_KV + p) * LANES
                o_buf[rows, sl:sl + LANES] = (acc[kvh][pr, :LANES] / den).astype(BF16)
        return carry

    lax.fori_loop(0, ATT_TQ // WINDOW, block, 0, unroll=True)
    o_ref[...] = x + _dot(o_buf[...], wo_ref[...])


def _swa(h, g, w_q, qg, sinks, cos_t, sin_t, kv, w_o, batch, seq):
    steps = seq // ATT_TQ
    blocks_per_step = ATT_TQ // WINDOW
    blocks_per_seq = seq // WINDOW

    def halo_map(b, s):
        return (jnp.maximum(b * blocks_per_seq + s * blocks_per_step - 1, 0), 0)

    return pl.pallas_call(
        _swa_kernel,
        out_shape=jax.ShapeDtypeStruct(h.shape, F32),
        grid=(batch, steps),
        in_specs=[
            pl.BlockSpec(memory_space=pltpu.SMEM),
            pl.BlockSpec((ATT_TQ, D_MODEL), lambda b, s: (b * steps + s, 0)),
            _resident((1, D_MODEL)),
            _resident((D_MODEL, D_MODEL)),
            _resident((1, LANES)),
            pl.BlockSpec((ATT_TQ, LANES), lambda b, s: (s, 0)),
            pl.BlockSpec((ATT_TQ, LANES), lambda b, s: (s, 0)),
            pl.BlockSpec((WINDOW, KV_LANES), halo_map),
            pl.BlockSpec((ATT_TQ, KV_LANES), lambda b, s: (b * steps + s, 0)),
            _resident((D_MODEL, D_MODEL)),
        ],
        out_specs=pl.BlockSpec((ATT_TQ, D_MODEL), lambda b, s: (b * steps + s, 0)),
        scratch_shapes=[
            pltpu.VMEM((ATT_TQ, D_MODEL), BF16),
            pltpu.VMEM((ATT_TQ + WINDOW, 4 * LANES), BF16),
            pltpu.VMEM((ATT_TQ + WINDOW, 8 * LANES), BF16),
            pltpu.VMEM((ATT_TQ, D_MODEL), BF16),
        ],
        compiler_params=pltpu.CompilerParams(
            dimension_semantics=("parallel", "parallel"), vmem_limit_bytes=VMEM_LIMIT),
        name="swa",
    )(sinks, h, g, w_q, qg, cos_t, sin_t, kv, kv, w_o)


def _rope_tables(seq):
    half = ROT_DIM // 2
    inv_freq = jnp.power(ROPE_THETA, -jnp.arange(half, dtype=F32) * (2.0 / ROT_DIM))
    ang = jnp.arange(seq).astype(F32)[:, None] * inv_freq[None, :]
    cos, sin = jnp.cos(ang), jnp.sin(ang)
    zeros = jnp.zeros_like(sin)
    rest = ATT_HEAD_DIM - ROT_DIM
    pad0 = jnp.zeros((seq, rest), F32)
    cos_t = jnp.concatenate([cos, cos, jnp.ones((seq, rest), F32)], axis=1)
    sin_lo = jnp.concatenate([-sin, zeros, pad0], axis=1)
    sin_hi = jnp.concatenate([zeros, sin, pad0], axis=1)
    sin_t = jnp.concatenate([sin, sin, pad0], axis=1)
    return tuple(jnp.tile(t, (1, HEADS_PER_SLAB)) for t in (cos_t, sin_lo, sin_hi, sin_t))


def kernel(x, ffn_norm_g, ffn_w_gate_up, ffn_w_down, mix_norm_g, hgrn_w_in, hgrn_lb_logits,
           hgrn_onorm_g, hgrn_w_out, kv_norm_g, kv_w, k_norm_g, attn_w_q, q_norm_g,
           attn_sinks, attn_w_out):
    batch, seq, d = x.shape
    assert d == D_MODEL and seq % ATT_TQ == 0 and seq % HG_TS == 0 and batch % HG_SEQS == 0
    assert (batch * seq) % FFN_TM == 0 and seq % FFN_TM == 0
    h = x.reshape(batch * seq, d)

    cos_t, sin_lo, sin_hi, sin_t = _rope_tables(seq)
    ffn_g = ffn_norm_g.reshape(ffn_norm_g.shape[0], 2, 1, D_MODEL)

    h, w_gu01, w_d01, w_in, w_hout = _ffn(
        h, ffn_g[0, 0], ffn_w_gate_up[0, 0].astype(BF16), ffn_w_down[0, 0].astype(BF16), "ffn_0_0",
        casts=((ffn_w_gate_up, (0, 1)), (ffn_w_down, (0, 1)), (hgrn_w_in, (0,)), (hgrn_w_out, (0,))))
    h = _hgrn(h.reshape(batch, seq, d), mix_norm_g[0].reshape(1, D_MODEL), w_in,
              hgrn_lb_logits, hgrn_onorm_g[0].reshape(1, HG_DV), w_hout).reshape(batch * seq, d)
    h, w_gu10, w_d10, w_kv, w_q, w_aout = _ffn(
        h, ffn_g[0, 1], w_gu01, w_d01, "ffn_0_1",
        casts=((ffn_w_gate_up, (1, 0)), (ffn_w_down, (1, 0)), (kv_w, ()), (attn_w_q, (0,)),
               (attn_w_out, (0,))))
    h, kv, w_gu11, w_d11 = _ffn(
        h, ffn_g[1, 0], w_gu10, w_d10, "ffn_1_0",
        kv_args=(kv_norm_g.reshape(1, D_MODEL), w_kv,
                 jnp.tile(k_norm_g, HEADS_PER_SLAB).reshape(1, LANES),
                 cos_t, sin_lo, sin_hi, seq),
        casts=((ffn_w_gate_up, (1, 1)), (ffn_w_down, (1, 1))))
    h = _swa(h, mix_norm_g[1].reshape(1, D_MODEL), w_q,
             jnp.tile(q_norm_g[0], HEADS_PER_SLAB).reshape(1, LANES), attn_sinks[0],
             cos_t, sin_t, kv, w_aout, batch, seq)
    (h,) = _ffn(h, ffn_g[1, 1], w_gu11, w_d11, "ffn_1_1")
    return h.reshape(batch, seq, d)
```
